```python
import jax, jax.numpy as jnp
from jax import lax
import numpy as np

D_MODEL = 1024
BATCH = 32
SEQ = 2048
DEPTH = 1

CTX_LEN = 256
GRID_W = 64
EPS = 1e-6

POOL_WINDOWS = (2, 4, 8, 16)
POOL_WIDTH = D_MODEL // 2
POOL_GROUP = POOL_WIDTH // len(POOL_WINDOWS)

RET_HEADS = 4
RET_DK = 128
RET_DV = 256
RET_CHUNK = 128
ROPE_BASE = 10000.0
QK_WIDTH = RET_HEADS * RET_DK
V_WIDTH = RET_HEADS * RET_DV

OFF_POOL = 0
OFF_Q = OFF_POOL + POOL_WIDTH
OFF_K = OFF_Q + QK_WIDTH
OFF_V = OFF_K + QK_WIDTH
OFF_G = OFF_V + V_WIDTH
OFF_MERGE = OFF_G + V_WIDTH
IN_WIDTH = OFF_MERGE + 2 * D_MODEL

PEER_HEADS = 8
PEER_NKEYS = 128
PEER_EXPERTS = PEER_NKEYS * PEER_NKEYS
PEER_TOPK = 16
PEER_DQ = 256
PEER_BLOCK = 128

kernel_name = "hybrid_pool_retention_peer_dit"


def rmsnorm(x, g):
    xf = x.astype(jnp.float32)
    y = xf * lax.rsqrt(jnp.mean(xf * xf, axis=-1, keepdims=True) + EPS)
    return (y * g.astype(jnp.float32)).astype(x.dtype)


def modulate(h, shift, scale):
    return h * (1.0 + scale) + shift


def heads(a, d):
    b, l, _ = a.shape
    return a.reshape(b, l, -1, d).transpose(0, 2, 1, 3)


def axial_rope(a, row, col):
    quarter = a.shape[-1] // 4
    inv = ROPE_BASE ** (-jnp.arange(quarter, dtype=jnp.float32) / quarter)
    ang = jnp.concatenate([row[:, None] * inv, col[:, None] * inv], axis=-1)
    cos, sin = jnp.cos(ang), jnp.sin(ang)
    a1, a2 = jnp.split(a, 2, axis=-1)
    return jnp.concatenate([a1 * cos - a2 * sin, a1 * sin + a2 * cos], axis=-1).astype(a.dtype)


def multiscale_pool(p, pool_w, pool_scale):
    L = p.shape[1]
    pf = p.astype(jnp.float32)
    cs = jnp.pad(jnp.cumsum(pf, axis=1), ((0, 0), (1, 0), (0, 0)))
    t = jnp.arange(L)
    groups = []
    for gi, w in enumerate(POOL_WINDOWS):
        lo = jnp.clip(t - w // 2, 0, L)
        hi = jnp.clip(t + w - w // 2, 0, L)
        sl = slice(gi * POOL_GROUP, (gi + 1) * POOL_GROUP)
        win_sum = cs[:, hi, sl] - cs[:, lo, sl]
        cnt = (hi - lo).astype(jnp.float32)[None, :, None]
        groups.append(win_sum / cnt - pf[:, :, sl])
    d = jnp.stack(groups, axis=2).astype(p.dtype)
    mixed = jnp.einsum('blgc,gcd->blgd', d, pool_w).reshape(p.shape)
    return mixed * pool_scale


def retention_scan(q, k, v, log_gamma, s0, strict):
    b, h, L, _ = q.shape
    dv = v.shape[-1]
    C = RET_CHUNK
    n = L // C

    def chunks(a):
        return a.reshape(b, h, n, C, a.shape[-1]).transpose(2, 0, 1, 3, 4)

    idx = jnp.arange(C, dtype=jnp.float32)
    rel = idx[:, None] - idx[None, :]
    mask = (rel > 0) if strict else (rel >= 0)
    lg = log_gamma[:, None, None]
    intra = jnp.where(mask, jnp.exp(lg * jnp.where(mask, rel, 0.0)), 0.0)
    q_dec = jnp.exp(log_gamma[:, None] * (idx + 1.0))[..., None]
    k_dec = jnp.exp(log_gamma[:, None] * (C - 1.0 - idx))[..., None]
    blk_dec = jnp.exp(log_gamma * C)[:, None, None]

    def step(s, qkv):
        qi, ki, vi = qkv
        scores = jnp.einsum('bhnd,bhmd->bhnm', qi, ki) * intra
        y = (jnp.einsum('bhnm,bhme->bhne', scores, vi)
             + jnp.einsum('bhnd,bhde->bhne', qi * q_dec, s))
        s = s * blk_dec + jnp.einsum('bhmd,bhme->bhde', ki * k_dec, vi)
        return s, y

    _, ys = lax.scan(step, s0, (chunks(q), chunks(k), chunks(v)))
    return ys.transpose(1, 2, 0, 3, 4).reshape(b, h, L, dv)


def context_states(kc, vc, lg):
    Lc = kc.shape[2]
    m = jnp.arange(Lc, dtype=jnp.float32)
    w_f = jnp.exp(lg[0][:, None] * (Lc - 1.0 - m))
    w_b = jnp.exp(lg[1][:, None] * m)
    kf = kc.astype(jnp.float32)
    vf = vc.astype(jnp.float32)
    s_f = jnp.einsum('hm,bhmd,bhme->bhde', w_f, kf, vf)
    s_b = jnp.einsum('hm,bhmd,bhme->bhde', w_b, kf, vf)
    return s_f, s_b


def bidir_retention(q, k, v, lg, s_f, s_b):
    y_f = retention_scan(q, k, v, lg[0], s_f, False)
    y_b = retention_scan(jnp.flip(q, 2), jnp.flip(k, 2), jnp.flip(v, 2), lg[1], s_b, True)
    return y_f + jnp.flip(y_b, 2)


def retention_out(y, gate, norm_g, w_o):
    b, h, L, dv = y.shape
    yf = y.astype(jnp.float32)
    mu = jnp.mean(yf, axis=-1, keepdims=True)
    var = jnp.mean(jnp.square(yf - mu), axis=-1, keepdims=True)
    yn = ((yf - mu) * lax.rsqrt(var + EPS)).transpose(0, 2, 1, 3).reshape(b, L, h * dv)
    yn = (yn * norm_g.astype(jnp.float32)).astype(gate.dtype)
    return (yn * jax.nn.silu(gate)) @ w_o


def mix_tokens(proj, y_ret, lp):
    pool = multiscale_pool(proj[..., OFF_POOL:OFF_Q], lp['pool_w'], lp['pool_scale']) @ lp['pool_out']
    ret = retention_out(y_ret, proj[..., OFF_G:OFF_MERGE], lp['ret_norm_g'], lp['ret_out'])
    g_pool, g_ret = jnp.split(proj[..., OFF_MERGE:], 2, axis=-1)
    merged = jax.nn.sigmoid(g_pool) * pool + jax.nn.sigmoid(g_ret) * ret
    return merged @ lp['w_out']


def peer_ffn(h, wq, keys, u, v):
    b, L, d = h.shape
    blocks = h.reshape(-1, PEER_BLOCK, d)

    def one_block(hb):
        t = hb.shape[0]
        q = (hb @ wq).reshape(t, PEER_HEADS, 2, PEER_DQ // 2)
        s = jnp.einsum('thpd,hpkd->thpk', q, keys).astype(jnp.float32)
        s1, i1 = lax.top_k(s[:, :, 0], PEER_TOPK)
        s2, i2 = lax.top_k(s[:, :, 1], PEER_TOPK)
        cand = (s1[..., :, None] + s2[..., None, :]).reshape(t, PEER_HEADS, PEER_TOPK * PEER_TOPK)
        cidx = (i1[..., :, None] * PEER_NKEYS + i2[..., None, :]).reshape(t, PEER_HEADS, PEER_TOPK * PEER_TOPK)
        best, pos = lax.top_k(cand, PEER_TOPK)
        eidx = jnp.take_along_axis(cidx, pos, axis=-1)
        g = jax.nn.softmax(best, axis=-1)
        act = jax.nn.gelu(jnp.einsum('thkd,td->thk', u[eidx], hb).astype(jnp.float32), approximate=False)
        return jnp.einsum('thk,thkd->td', (g * act).astype(hb.dtype), v[eidx])

    return lax.map(one_block, blocks).reshape(b, L, d)


def trunk_layer(x, ctx, c, c_ctx, lp, update_ctx):
    b, L, d = x.shape
    mod = jax.nn.silu(c) @ lp['ada_w'] + lp['ada_b']
    mod_c = jax.nn.silu(c_ctx) @ lp['ada_w'] + lp['ada_b']
    sh1, sc1, g1, sh2, sc2, g2 = jnp.split(mod[:, None, :], 6, axis=-1)
    csh1, csc1, cg1, csh2, csc2, cg2 = jnp.split(mod_c, 6, axis=-1)
    lg = jax.nn.log_sigmoid(lp['ret_decay'].astype(jnp.float32))
    k_scale = RET_DK ** -0.5

    rows = L // GRID_W
    row = jnp.repeat(jnp.arange(rows, dtype=jnp.float32), GRID_W)
    col = jnp.tile(jnp.arange(GRID_W, dtype=jnp.float32), rows)

    hc = modulate(rmsnorm(ctx, lp['norm_mix_g']), csh1, csc1)
    if update_ctx:
        projc = hc @ lp['w_in']
        kv_c = projc[..., OFF_K:OFF_G]
    else:
        kv_c = hc @ lp['w_in'][:, OFF_K:OFF_G]
    kc = heads(kv_c[..., :QK_WIDTH], RET_DK) * k_scale
    vc = heads(kv_c[..., QK_WIDTH:], RET_DV)
    s_f, s_b = context_states(kc, vc, lg)

    h = modulate(rmsnorm(x, lp['norm_mix_g']), sh1, sc1)
    proj = h @ lp['w_in']
    q = axial_rope(heads(proj[..., OFF_Q:OFF_K], RET_DK), row, col)
    k = axial_rope(heads(proj[..., OFF_K:OFF_V], RET_DK), row, col) * k_scale
    v = heads(proj[..., OFF_V:OFF_G], RET_DV)
    y = bidir_retention(q, k, v, lg, s_f, s_b)
    x = x + g1 * mix_tokens(proj, y, lp)

    hf = modulate(rmsnorm(x, lp['norm_ffn_g']), sh2, sc2)
    x = x + g2 * peer_ffn(hf, lp['peer_wq'], lp['peer_keys'], lp['peer_u'], lp['peer_v'])

    if update_ctx:
        qc = heads(projc[..., OFF_Q:OFF_K], RET_DK)
        zero = jnp.zeros_like(s_f)
        yc = bidir_retention(qc, kc, vc, lg, zero, zero)
        ctx = ctx + cg1 * mix_tokens(projc, yc, lp)
        hcf = modulate(rmsnorm(ctx, lp['norm_ffn_g']), csh2, csc2)
        ctx = ctx + cg2 * peer_ffn(hcf, lp['peer_wq'], lp['peer_keys'], lp['peer_u'], lp['peer_v'])
    return x, ctx


def setup_inputs(seed: int = 0) -> dict:
    key = jax.random.key(seed)
    ks = jax.random.split(key, 24)
    f32 = jnp.float32
    D = D_MODEL

    def nrm(k, shape, s):
        return jax.random.normal(k, shape, f32) * s

    base_logit = jnp.log(2.0 ** (5.0 + jnp.arange(RET_HEADS, dtype=f32)) - 1.0)
    return {
        'x': nrm(ks[0], (BATCH, SEQ, D), 1.0),
        'c': nrm(ks[1], (BATCH, D), 1.0),
        'ctx': nrm(ks[2], (BATCH, CTX_LEN, D), 1.0),
        'c_ctx': nrm(ks[3], (D,), 1.0),
        'ada_w': nrm(ks[4], (DEPTH, D, 6 * D), 0.5 * D ** -0.5),
        'ada_b': nrm(ks[5], (DEPTH, 6 * D), 0.02),
        'norm_mix_g': 1.0 + nrm(ks[6], (DEPTH, D), 0.02),
        'norm_ffn_g': 1.0 + nrm(ks[7], (DEPTH, D), 0.02),
        'w_in': nrm(ks[8], (DEPTH, D, IN_WIDTH), D ** -0.5),
        'pool_w': nrm(ks[9], (DEPTH, len(POOL_WINDOWS), POOL_GROUP, POOL_GROUP), POOL_GROUP ** -0.5),
        'pool_scale': 1.0 + nrm(ks[10], (DEPTH, POOL_WIDTH), 0.1),
        'pool_out': nrm(ks[11], (DEPTH, POOL_WIDTH, D), POOL_WIDTH ** -0.5),
        'ret_decay': base_logit[None, None, :] + nrm(ks[12], (DEPTH, 2, RET_HEADS), 0.1),
        'ret_norm_g': 1.0 + nrm(ks[13], (DEPTH, V_WIDTH), 0.02),
        'ret_out': nrm(ks[14], (DEPTH, V_WIDTH, D), V_WIDTH ** -0.5),
        'w_out': nrm(ks[15], (DEPTH, D, D), D ** -0.5),
        'peer_wq': nrm(ks[16], (DEPTH, D, PEER_HEADS * PEER_DQ), D ** -0.5),
        'peer_keys': nrm(ks[17], (DEPTH, PEER_HEADS, 2, PEER_NKEYS, PEER_DQ // 2), (PEER_DQ // 2) ** -0.5),
        'peer_u': nrm(ks[18], (DEPTH, PEER_EXPERTS, D), D ** -0.5),
        'peer_v': nrm(ks[19], (DEPTH, PEER_EXPERTS, D), 0.5),
        'final_g': 1.0 + nrm(ks[20], (D,), 0.02),
    }


def reference(x, c, ctx, c_ctx, ada_w, ada_b, norm_mix_g, norm_ffn_g, w_in, pool_w,
              pool_scale, pool_out, ret_decay, ret_norm_g, ret_out, w_out, peer_wq,
              peer_keys, peer_u, peer_v, final_g):
    for layer in range(DEPTH):
        lp = {
            'ada_w': ada_w[layer], 'ada_b': ada_b[layer],
            'norm_mix_g': norm_mix_g[layer], 'norm_ffn_g': norm_ffn_g[layer],
            'w_in': w_in[layer], 'pool_w': pool_w[layer], 'pool_scale': pool_scale[layer],
            'pool_out': pool_out[layer], 'ret_decay': ret_decay[layer],
            'ret_norm_g': ret_norm_g[layer], 'ret_out': ret_out[layer], 'w_out': w_out[layer],
            'peer_wq': peer_wq[layer], 'peer_keys': peer_keys[layer],
            'peer_u': peer_u[layer], 'peer_v': peer_v[layer],
        }
        x, ctx = trunk_layer(x, ctx, c, c_ctx, lp, layer + 1 < DEPTH)
    return rmsnorm(x, final_g)
```

```python
import functools

import jax
import jax.numpy as jnp
import numpy as np
from jax import lax
from jax.experimental import pallas as pl
from jax.experimental.pallas import tpu as pltpu

F32 = jnp.float32
BF16 = jnp.bfloat16
I32 = jnp.int32

D_MODEL = 1024
GRID_W = 64
EPS = 1e-6

POOL_WINDOWS = (2, 4, 8, 16)
POOL_WIDTH = D_MODEL // 2
POOL_GROUP = POOL_WIDTH // len(POOL_WINDOWS)

RET_HEADS = 4
RET_DK = 128
RET_DV = 256
RET_CHUNK = 128
ROPE_BASE = 10000.0
QK_WIDTH = RET_HEADS * RET_DK
V_WIDTH = RET_HEADS * RET_DV
K_SCALE = RET_DK ** -0.5

OFF_POOL = 0
OFF_Q = OFF_POOL + POOL_WIDTH
OFF_K = OFF_Q + QK_WIDTH
OFF_V = OFF_K + QK_WIDTH
OFF_G = OFF_V + V_WIDTH
OFF_MERGE = OFF_G + V_WIDTH
IN_WIDTH = OFF_MERGE + 2 * D_MODEL

PEER_HEADS = 8
PEER_NKEYS = 128
PEER_EXPERTS = PEER_NKEYS * PEER_NKEYS
PEER_TOPK = 16
PEER_DQ = 256
PEER_SLOTS = PEER_HEADS * PEER_TOPK

LANES = 128
SUBLANES = 8
ROW_TILES = D_MODEL // LANES
VMEM_LIMIT = 56 * 1024 * 1024

_NT = (((1,), (1,)), ((), ()))
_TN = (((0,), (0,)), ((), ()))


def _params(sem, vmem=None):
    return pltpu.CompilerParams(dimension_semantics=sem, vmem_limit_bytes=vmem)


def _resident(shape):
    nd = len(shape)
    return pl.BlockSpec(shape, lambda *_: (0,) * nd, pipeline_mode=pl.Buffered(1))


def _rms_mod(xf, g, shift, scale):
    y = xf * lax.rsqrt(jnp.mean(xf * xf, axis=-1, keepdims=True) + EPS)
    return (y * g) * (1.0 + scale) + shift


def _ada_body(c_ref, w_ref, b_ref, o_ref):
    c = c_ref[...]
    s = c * jax.nn.sigmoid(c)
    o_ref[...] = jnp.dot(s, w_ref[...], preferred_element_type=F32,
                         precision=lax.Precision.HIGHEST) + b_ref[...]


def ada_stage(cc, ada_w, ada_b, tn=512):
    r, d = cc.shape
    n = ada_w.shape[1]
    return pl.pallas_call(
        _ada_body,
        grid=(n // tn,),
        in_specs=[pl.BlockSpec((r, d), lambda j: (0, 0)),
                  pl.BlockSpec((d, tn), lambda j: (0, j)),
                  pl.BlockSpec((1, tn), lambda j: (0, j))],
        out_specs=pl.BlockSpec((r, tn), lambda j: (0, j)),
        out_shape=jax.ShapeDtypeStruct((r, n), F32),
        compiler_params=_params(("parallel",)),
        name="ada",
    )(cc, ada_w, ada_b.reshape(1, n))


def _ctx_body(lg_ref, ctx_ref, g_ref, sh_ref, sc_ref, w_ref, sf_ref, sb_ref):
    lc = ctx_ref.shape[0]
    hc = _rms_mod(ctx_ref[...], g_ref[...], sh_ref[...], sc_ref[...])
    kv = jnp.dot(hc.astype(BF16), w_ref[...], preferred_element_type=F32)
    m = lax.broadcasted_iota(I32, (lc, RET_DK), 0).astype(F32)
    for h in range(RET_HEADS):
        kf = kv[:, h * RET_DK:(h + 1) * RET_DK] * K_SCALE
        vb = kv[:, QK_WIDTH + h * RET_DV:QK_WIDTH + (h + 1) * RET_DV].astype(BF16)
        wf = jnp.exp(lg_ref[0, h] * (lc - 1.0 - m))
        wb = jnp.exp(lg_ref[1, h] * m)
        sf_ref[h] = lax.dot_general((kf * wf).astype(BF16), vb, _TN, preferred_element_type=F32)
        sb_ref[h] = lax.dot_general((kf * wb).astype(BF16), vb, _TN, preferred_element_type=F32)


def ctx_stage(lg, ctx, norm_g, csh, csc, w_kv):
    b, lc, d = ctx.shape
    vec = pl.BlockSpec((1, d), lambda i: (0, 0))
    st = jax.ShapeDtypeStruct((b, RET_HEADS, RET_DK, RET_DV), F32)
    st_spec = pl.BlockSpec((None, RET_HEADS, RET_DK, RET_DV), lambda i: (i, 0, 0, 0))
    return pl.pallas_call(
        _ctx_body,
        grid=(b,),
        in_specs=[pl.BlockSpec(memory_space=pltpu.SMEM),
                  pl.BlockSpec((None, lc, d), lambda i: (i, 0, 0)),
                  vec, vec, vec,
                  pl.BlockSpec(w_kv.shape, lambda i: (0, 0))],
        out_specs=[st_spec, st_spec],
        out_shape=[st, st],
        compiler_params=_params(("parallel",)),
        name="ctx",
    )(lg, ctx, norm_g.reshape(1, d), csh.reshape(1, d), csc.reshape(1, d), w_kv)


def _rope(a, cos, sin_signed):
    return a * cos + pltpu.roll(a, RET_DK // 2, 1) * sin_signed


def _proj_body(x_ref, g_ref, sh_ref, sc_ref, cos_ref, sin_ref, w_ref,
               p_ref, q_ref, k_ref, v_ref, gz_ref, mg_ref):
    h = _rms_mod(x_ref[...], g_ref[...], sh_ref[...], sc_ref[...]).astype(BF16)

    def mm(lo, hi):
        return jnp.dot(h, w_ref[:, lo:hi], preferred_element_type=F32)

    p_ref[...] = mm(OFF_POOL, OFF_Q)
    cos = cos_ref[...]
    sin = sin_ref[...]
    qf = mm(OFF_Q, OFF_K)
    kf = mm(OFF_K, OFF_V)
    for hd in range(RET_HEADS):
        sl = slice(hd * RET_DK, (hd + 1) * RET_DK)
        q_ref[:, sl] = _rope(qf[:, sl], cos, sin).astype(BF16)
        k_ref[:, sl] = (_rope(kf[:, sl], cos, sin) * K_SCALE).astype(BF16)
    v_ref[...] = mm(OFF_V, OFF_G).astype(BF16)
    gz_ref[...] = mm(OFF_G, OFF_MERGE).astype(BF16)
    mg_ref[...] = mm(OFF_MERGE, IN_WIDTH).astype(BF16)


def proj_stage(x, norm_g, sh1, sc1, cos, sin, w_in_bf, tm=512):
    b, l, d = x.shape
    vec_b = pl.BlockSpec((None, 1, d), lambda i, j: (i, 0, 0))
    rows = lambda w: pl.BlockSpec((None, tm, w), lambda i, j: (i, j, 0))
    tab = pl.BlockSpec((tm, RET_DK), lambda i, j: (j, 0))
    outs = [(POOL_WIDTH, F32), (QK_WIDTH, BF16), (QK_WIDTH, BF16), (V_WIDTH, BF16),
            (V_WIDTH, BF16), (2 * D_MODEL, BF16)]
    return pl.pallas_call(
        _proj_body,
        grid=(b, l // tm),
        in_specs=[rows(d), pl.BlockSpec((1, d), lambda i, j: (0, 0)), vec_b, vec_b, tab, tab,
                  _resident(w_in_bf.shape)],
        out_specs=[rows(w) for w, _ in outs],
        out_shape=[jax.ShapeDtypeStruct((b, l, w), dt) for w, dt in outs],
        compiler_params=_params(("parallel", "parallel"), VMEM_LIMIT),
        name="proj",
    )(x, norm_g.reshape(1, d), sh1, sc1, cos, sin, w_in_bf)


def _ret_body(lg_ref, q_ref, k_ref, v_ref, sf_ref, sb_ref, y_ref, s_scr):
    c = RET_CHUNK
    n_chunks = q_ref.shape[0] // c
    hd = pl.program_id(1)
    lgf = lg_ref[0, hd]
    lgb = lg_ref[1, hd]
    n_i = lax.broadcasted_iota(I32, (c, c), 0)
    m_i = lax.broadcasted_iota(I32, (c, c), 1)
    rel = (n_i - m_i).astype(F32)
    intra_f = jnp.where(rel >= 0, jnp.exp(lgf * jnp.where(rel >= 0, rel, 0.0)), 0.0)
    intra_b = jnp.where(rel < 0, jnp.exp(lgb * jnp.where(rel < 0, -rel, 0.0)), 0.0)
    pos = lax.broadcasted_iota(I32, (c, RET_DK), 0).astype(F32)
    qdec_f = jnp.exp(lgf * (pos + 1.0))
    kdec_f = jnp.exp(lgf * (c - 1.0 - pos))
    qdec_b = jnp.exp(lgb * (c - pos))
    kdec_b = jnp.exp(lgb * pos)
    blk_f = jnp.exp(jnp.full((1, RET_DV), lgf * c, F32))
    blk_b = jnp.exp(jnp.full((1, RET_DV), lgb * c, F32))

    def chunk(i, intra, qdec, kdec, blk):
        rows = pl.ds(pl.multiple_of(i * c, c), c)
        qi = q_ref[rows, :]
        ki = k_ref[rows, :]
        vi = v_ref[rows, :]
        sc = lax.dot_general(qi, ki, _NT, preferred_element_type=F32) * intra
        s = s_scr[...]
        y = (jnp.dot(sc.astype(BF16), vi, preferred_element_type=F32)
             + jnp.dot((qi.astype(F32) * qdec).astype(BF16), s.astype(BF16),
                       preferred_element_type=F32))
        s_scr[...] = s * blk + lax.dot_general((ki.astype(F32) * kdec).astype(BF16), vi, _TN,
                                               preferred_element_type=F32)
        return rows, y

    s_scr[...] = sf_ref[...]

    def fwd(i, carry):
        rows, y = chunk(i, intra_f, qdec_f, kdec_f, blk_f)
        y_ref[rows, :] = y
        return carry

    lax.fori_loop(0, n_chunks, fwd, 0)
    s_scr[...] = sb_ref[...]

    def bwd(j, carry):
        rows, y = chunk(n_chunks - 1 - j, intra_b, qdec_b, kdec_b, blk_b)
        y_ref[rows, :] += y
        return carry

    lax.fori_loop(0, n_chunks, bwd, 0)


def ret_stage(lg, q, k, v, s_f, s_b):
    b, l, _ = q.shape
    qk_spec = pl.BlockSpec((None, l, RET_DK), lambda i, j: (i, 0, j))
    v_spec = pl.BlockSpec((None, l, RET_DV), lambda i, j: (i, 0, j))
    st_spec = pl.BlockSpec((None, None, RET_DK, RET_DV), lambda i, j: (i, j, 0, 0))
    return pl.pallas_call(
        _ret_body,
        grid=(b, RET_HEADS),
        in_specs=[pl.BlockSpec(memory_space=pltpu.SMEM), qk_spec, qk_spec, v_spec, st_spec, st_spec],
        out_specs=v_spec,
        out_shape=jax.ShapeDtypeStruct((b, l, V_WIDTH), F32),
        scratch_shapes=[pltpu.VMEM((RET_DK, RET_DV), F32)],
        compiler_params=_params(("parallel", "parallel")),
        name="ret",
    )(lg, q, k, v, s_f, s_b)


def _pool_bands(tm):
    r = np.arange(tm)[:, None]
    c = np.arange(tm)[None, :]
    bands = np.zeros((len(POOL_WINDOWS), 3, tm, tm), np.float32)
    for gi, w in enumerate(POOL_WINDOWS):
        lo, hi = r - w // 2, r + w - w // 2
        for j, off in enumerate((-tm, 0, tm)):
            bands[gi, j] = ((c + off >= lo) & (c + off < hi)).astype(np.float32)
    return jnp.asarray(bands, BF16)


def _mix_body(pp_ref, pm_ref, pn_ref, y_ref, gz_ref, mg_ref, x_ref, g1_ref, sh2_ref, sc2_ref,
              band_ref, pw_ref, ps_ref, po_ref, rg_ref, ro_ref, wo_ref, ng_ref,
              x1_ref, hf_ref, *, seq_len):
    tm = pm_ref.shape[0]
    li = pl.program_id(1)
    has_prev = (li > 0).astype(F32)
    has_next = (li < pl.num_programs(1) - 1).astype(F32)
    t = li * tm + lax.broadcasted_iota(I32, (tm, POOL_GROUP), 0)

    def window_sum(ref, cols, gi, j):
        pf = ref[:, cols]
        hi = pf.astype(BF16)
        lo = (pf - hi.astype(F32)).astype(BF16)
        band = band_ref[gi, j]
        return (jnp.dot(band, hi, preferred_element_type=F32)
                + jnp.dot(band, lo, preferred_element_type=F32))

    mixed = []
    for gi, w in enumerate(POOL_WINDOWS):
        cols = slice(gi * POOL_GROUP, (gi + 1) * POOL_GROUP)
        ws = (window_sum(pm_ref, cols, gi, 1) + has_prev * window_sum(pp_ref, cols, gi, 0)
              + has_next * window_sum(pn_ref, cols, gi, 2))
        cnt = (jnp.clip(t + (w - w // 2), 0, seq_len) - jnp.clip(t - w // 2, 0, seq_len)).astype(F32)
        dgi = ws / cnt - pm_ref[:, cols]
        mixed.append(jnp.dot(dgi.astype(BF16), pw_ref[gi], preferred_element_type=F32))
    mixed = jnp.concatenate(mixed, axis=1) * ps_ref[...]
    pool = jnp.dot(mixed.astype(BF16), po_ref[...], preferred_element_type=F32)

    yn = []
    for hd in range(RET_HEADS):
        yh = y_ref[:, hd * RET_DV:(hd + 1) * RET_DV]
        mu = jnp.mean(yh, axis=-1, keepdims=True)
        yc = yh - mu
        var = jnp.mean(yc * yc, axis=-1, keepdims=True)
        yn.append(yc * lax.rsqrt(var + EPS))
    yn = jnp.concatenate(yn, axis=1) * rg_ref[...]
    gate = gz_ref[...].astype(F32)
    ret = jnp.dot((yn * (gate * jax.nn.sigmoid(gate))).astype(BF16), ro_ref[...],
                  preferred_element_type=F32)

    g_pool = mg_ref[:, :D_MODEL].astype(F32)
    g_ret = mg_ref[:, D_MODEL:].astype(F32)
    merged = jax.nn.sigmoid(g_pool) * pool + jax.nn.sigmoid(g_ret) * ret
    out = jnp.dot(merged.astype(BF16), wo_ref[...], preferred_element_type=F32)
    x1 = x_ref[...] + g1_ref[...] * out
    x1_ref[...] = x1
    hf_ref[...] = _rms_mod(x1, ng_ref[...], sh2_ref[...], sc2_ref[...])


def mix_stage(p, y, gz, mg, x, g1, sh2, sc2, pool_w_bf, pool_scale, pool_out_bf, ret_norm_g,
              ret_out_bf, w_out_bf, norm_ffn_g, tm=256):
    b, l, d = x.shape
    nl = l // tm
    bands = _pool_bands(tm)
    rows = lambda w: pl.BlockSpec((None, tm, w), lambda i, j: (i, j, 0))
    vec_b = pl.BlockSpec((None, 1, d), lambda i, j: (i, 0, 0))
    const = lambda a: pl.BlockSpec(a.shape, lambda i, j: (0,) * a.ndim)
    ps = pool_scale.reshape(1, POOL_WIDTH)
    rg = ret_norm_g.reshape(1, V_WIDTH)
    ng = norm_ffn_g.reshape(1, d)
    return pl.pallas_call(
        functools.partial(_mix_body, seq_len=l),
        grid=(b, nl),
        in_specs=[pl.BlockSpec((None, tm, POOL_WIDTH), lambda i, j: (i, jnp.maximum(j - 1, 0), 0)),
                  rows(POOL_WIDTH),
                  pl.BlockSpec((None, tm, POOL_WIDTH), lambda i, j: (i, jnp.minimum(j + 1, nl - 1), 0)),
                  rows(V_WIDTH), rows(V_WIDTH), rows(2 * D_MODEL), rows(d), vec_b, vec_b, vec_b,
                  const(bands), const(pool_w_bf), const(ps), const(pool_out_bf), const(rg),
                  const(ret_out_bf), const(w_out_bf), const(ng)],
        out_specs=[rows(d), rows(d)],
        out_shape=[jax.ShapeDtypeStruct((b, l, d), F32), jax.ShapeDtypeStruct((b, l, d), F32)],
        compiler_params=_params(("parallel", "parallel"), VMEM_LIMIT),
        name="mix",
    )(p, p, p, y, gz, mg, x, g1, sh2, sc2, bands, pool_w_bf, ps, pool_out_bf, rg, ret_out_bf,
      w_out_bf, ng)


def _topk_rows(s, k):
    r, n = s.shape
    rows = lax.broadcasted_iota(I32, (r, n), 0)
    slot = lax.broadcasted_iota(I32, (k, n), 0)
    vals = jnp.zeros((k, n), F32)
    idxs = jnp.zeros((k, n), I32)
    for j in range(k):
        m = jnp.max(s, axis=0, keepdims=True)
        i = jnp.min(jnp.where(s == m, rows, r), axis=0, keepdims=True)
        vals = jnp.where(slot == j, m, vals)
        idxs = jnp.where(slot == j, i, idxs)
        s = jnp.where(rows == i, -jnp.inf, s)
    return vals, idxs


def _pick_rows(table, sel):
    out = jnp.zeros(sel.shape, table.dtype)
    for r in range(table.shape[0]):
        out = jnp.where(sel == r, table[r:r + 1, :], out)
    return out


def _route_body(hf_ref, wq_ref, keys_ref, e_ref, g_ref):
    tq = hf_ref.shape[0]
    half = PEER_DQ // 2
    q = jnp.dot(hf_ref[...].astype(BF16), wq_ref[...], preferred_element_type=F32).astype(BF16)
    for cb in range(tq // LANES):
        tok = slice(cb * LANES, (cb + 1) * LANES)
        e_heads, g_heads = [], []
        for hd in range(PEER_HEADS):
            sub = []
            for part in range(2):
                col = (hd * 2 + part) * half
                st = lax.dot_general(keys_ref[hd, part], q[tok, col:col + half], _NT,
                                     preferred_element_type=F32)
                sub.append(_topk_rows(st, PEER_TOPK))
            (s1, i1), (s2, i2) = sub
            cand = jnp.concatenate([s1[i:i + 1, :] + s2 for i in range(PEER_TOPK)], axis=0)
            best, pos = _topk_rows(cand, PEER_TOPK)
            e1 = _pick_rows(i1, pos >> 4)
            e2 = _pick_rows(i2, pos & (PEER_TOPK - 1))
            ex = jnp.exp(best - best[0:1, :])
            e_heads.append(e1 * PEER_NKEYS + e2)
            g_heads.append(ex / jnp.sum(ex, axis=0, keepdims=True))
        e_ref[tok, :] = jnp.concatenate(e_heads, axis=0).T
        g_ref[tok, :] = jnp.concatenate(g_heads, axis=0).T


def route_stage(hf, wq_bf, keys_bf, tq=256):
    n, d = hf.shape
    rows = lambda w: pl.BlockSpec((tq, w), lambda i: (i, 0))
    return pl.pallas_call(
        _route_body,
        grid=(n // tq,),
        in_specs=[rows(d), pl.BlockSpec(wq_bf.shape, lambda i: (0, 0)),
                  pl.BlockSpec(keys_bf.shape, lambda i: (0, 0, 0, 0))],
        out_specs=[rows(PEER_SLOTS), rows(PEER_SLOTS)],
        out_shape=[jax.ShapeDtypeStruct((n, PEER_SLOTS), I32),
                   jax.ShapeDtypeStruct((n, PEER_SLOTS), F32)],
        compiler_params=_params(("parallel",), VMEM_LIMIT),
        name="route",
    )(hf, wq_bf, keys_bf)


def _peer_u_body(e_ref, h_ref, g_ref, tbl_ref, w_ref, p_scr, a_scr):
    tb = h_ref.shape[0]
    lane = lax.broadcasted_iota(I32, (PEER_SLOTS, tb), 1)
    a_scr[...] = jnp.zeros_like(a_scr)

    def token(t, carry):
        hv = h_ref[t]
        for k in range(PEER_SLOTS):
            row = tbl_ref[e_ref[t, k]].astype(F32)
            p_scr[pl.ds(k * SUBLANES, SUBLANES), :] = row * hv
        parts = []
        for j in range(PEER_SLOTS // SUBLANES):
            base = j * SUBLANES * SUBLANES
            acc = p_scr[pl.ds(base, SUBLANES, stride=SUBLANES), :]
            for s in range(1, SUBLANES):
                acc = acc + p_scr[pl.ds(base + s, SUBLANES, stride=SUBLANES), :]
            parts.append(acc)
        col = jnp.sum(jnp.concatenate(parts, axis=0), axis=1, keepdims=True)
        a_scr[...] = jnp.where(lane == t, col, a_scr[...])
        return carry

    lax.fori_loop(0, tb, token, 0)
    act = a_scr[...].T
    w_ref[...] = g_ref[...] * (0.5 * act * (1.0 + lax.erf(act * (2.0 ** -0.5))))


def peer_u_stage(eidx, hf3, gates, tbl, tb=128):
    n = eidx.shape[0]
    return pl.pallas_call(
        _peer_u_body,
        grid=(n // tb,),
        in_specs=[pl.BlockSpec((tb, PEER_SLOTS), lambda i: (i, 0), memory_space=pltpu.SMEM),
                  pl.BlockSpec((tb, ROW_TILES, LANES), lambda i: (i, 0, 0)),
                  pl.BlockSpec((tb, PEER_SLOTS), lambda i: (i, 0)),
                  _resident(tbl.shape)],
        out_specs=pl.BlockSpec((tb, PEER_SLOTS), lambda i: (i, 0)),
        out_shape=jax.ShapeDtypeStruct((n, PEER_SLOTS), F32),
        scratch_shapes=[pltpu.VMEM((PEER_SLOTS * SUBLANES, LANES), F32),
                        pltpu.VMEM((PEER_SLOTS, tb), F32)],
        compiler_params=_params(("arbitrary",), VMEM_LIMIT),
        name="peer_u",
    )(eidx, hf3, gates, tbl)


_V_ACCS = 4


def _peer_v_body(e_ref, w_ref, x_ref, g2_ref, fg_ref, tbl_ref, o_ref):
    tb = x_ref.shape[0]
    g2 = g2_ref[...]
    fg = fg_ref[...]

    def token(t, carry):
        accs = [None] * _V_ACCS
        for k in range(PEER_SLOTS):
            term = w_ref[t, k] * tbl_ref[e_ref[t, k]].astype(F32)
            a = k % _V_ACCS
            accs[a] = term if accs[a] is None else accs[a] + term
        peer = (accs[0] + accs[1]) + (accs[2] + accs[3])
        x2 = x_ref[t] + g2 * peer
        ms = jnp.sum(jnp.sum(x2 * x2, axis=1, keepdims=True), axis=0, keepdims=True) / D_MODEL
        o_ref[t] = (x2 * lax.rsqrt(ms + EPS)) * fg
        return carry

    lax.fori_loop(0, tb, token, 0)


def peer_v_stage(eidx, w, x3, g2, final_g, tbl, tokens_per_batch, tb=128):
    n = eidx.shape[0]
    per_b = tokens_per_batch // tb
    smem = lambda: pl.BlockSpec((tb, PEER_SLOTS), lambda i: (i, 0), memory_space=pltpu.SMEM)
    tile = lambda: pl.BlockSpec((tb, ROW_TILES, LANES), lambda i: (i, 0, 0))
    return pl.pallas_call(
        _peer_v_body,
        grid=(n // tb,),
        in_specs=[smem(), smem(), tile(),
                  pl.BlockSpec((None, ROW_TILES, LANES), lambda i: (i // per_b, 0, 0)),
                  pl.BlockSpec((ROW_TILES, LANES), lambda i: (0, 0)),
                  _resident(tbl.shape)],
        out_specs=tile(),
        out_shape=jax.ShapeDtypeStruct((n, ROW_TILES, LANES), F32),
        compiler_params=_params(("arbitrary",), VMEM_LIMIT),
        name="peer_v",
    )(eidx, w, x3, g2, final_g.reshape(ROW_TILES, LANES), tbl)


def _rope_tables(l):
    quarter = RET_DK // 4
    rows = l // GRID_W
    row = jnp.repeat(jnp.arange(rows, dtype=F32), GRID_W)
    col = jnp.tile(jnp.arange(GRID_W, dtype=F32), rows)
    inv = ROPE_BASE ** (-jnp.arange(quarter, dtype=F32) / quarter)
    ang = jnp.concatenate([row[:, None] * inv, col[:, None] * inv], axis=-1)
    cos, sin = jnp.cos(ang), jnp.sin(ang)
    return jnp.concatenate([cos, cos], axis=-1), jnp.concatenate([-sin, sin], axis=-1)


def _layer(x, ctx, c, c_ctx, ada_w, ada_b, norm_mix_g, norm_ffn_g, w_in, pool_w, pool_scale,
           pool_out, ret_decay, ret_norm_g, ret_out, w_out, peer_wq, peer_keys, peer_u, peer_v,
           final_g):
    b, l, d = x.shape
    n = b * l

    rows = -(-(b + 1) // SUBLANES) * SUBLANES
    cc = jnp.zeros((rows, d), F32).at[:b].set(c).at[b].set(c_ctx)
    mod = ada_stage(cc, ada_w, ada_b)
    sh1, sc1, g1, sh2, sc2, g2 = [m.reshape(b, 1, d) for m in jnp.split(mod[:b], 6, axis=-1)]
    csh1, csc1 = mod[b, :d], mod[b, d:2 * d]
    lg = jax.nn.log_sigmoid(ret_decay.astype(F32))

    w_in_bf = w_in.astype(BF16)
    s_f, s_b = ctx_stage(lg, ctx, norm_mix_g, csh1, csc1, w_in_bf[:, OFF_K:OFF_G])

    cos, sin = _rope_tables(l)
    p, q, k, v, gz, mg = proj_stage(x, norm_mix_g, sh1, sc1, cos, sin, w_in_bf)
    y = ret_stage(lg, q, k, v, s_f, s_b)
    x1, hf = mix_stage(p, y, gz, mg, x, g1, sh2, sc2, pool_w.astype(BF16), pool_scale,
                       pool_out.astype(BF16), ret_norm_g, ret_out.astype(BF16),
                       w_out.astype(BF16), norm_ffn_g)

    hf2 = hf.reshape(n, d)
    eidx, gates = route_stage(hf2, peer_wq.astype(BF16), peer_keys.astype(BF16))
    u_tbl = peer_u.astype(BF16).reshape(PEER_EXPERTS, ROW_TILES, LANES)
    v_tbl = peer_v.astype(BF16).reshape(PEER_EXPERTS, ROW_TILES, LANES)
    w = peer_u_stage(eidx, hf2.reshape(n, ROW_TILES, LANES), gates, u_tbl)
    out = peer_v_stage(eidx, w, x1.reshape(n, ROW_TILES, LANES), g2.reshape(b, ROW_TILES, LANES),
                       final_g, v_tbl, l)
    return out.reshape(b, l, d)


def kernel(x, c, ctx, c_ctx, ada_w, ada_b, norm_mix_g, norm_ffn_g, w_in, pool_w, pool_scale, pool_out, ret_decay, ret_norm_g, ret_out, w_out, peer_wq, peer_keys, peer_u, peer_v, final_g):
    assert ada_w.shape[0] == 1, "single-layer block"
    return _layer(x, ctx, c, c_ctx, ada_w[0], ada_b[0], norm_mix_g[0], norm_ffn_g[0], w_in[0],
                  pool_w[0], pool_scale[0], pool_out[0], ret_decay[0], ret_norm_g[0], ret_out[0],
                  w_out[0], peer_wq[0], peer_keys[0], peer_u[0], peer_v[0], final_g)
```

```python
import functools

import jax
import jax.numpy as jnp
import numpy as np
from jax import lax
from jax.experimental import pallas as pl
from jax.experimental.pallas import tpu as pltpu

F32 = jnp.float32
BF16 = jnp.bfloat16
I32 = jnp.int32

D_MODEL = 1024
GRID_W = 64
EPS = 1e-6

POOL_WINDOWS = (2, 4, 8, 16)
POOL_WIDTH = D_MODEL // 2
POOL_GROUP = POOL_WIDTH // len(POOL_WINDOWS)

RET_HEADS = 4
RET_DK = 128
RET_DV = 256
RET_CHUNK = 128
ROPE_BASE = 10000.0
QK_WIDTH = RET_HEADS * RET_DK
V_WIDTH = RET_HEADS * RET_DV
K_SCALE = RET_DK ** -0.5

OFF_POOL = 0
OFF_Q = OFF_POOL + POOL_WIDTH
OFF_K = OFF_Q + QK_WIDTH
OFF_V = OFF_K + QK_WIDTH
OFF_G = OFF_V + V_WIDTH
OFF_MERGE = OFF_G + V_WIDTH
IN_WIDTH = OFF_MERGE + 2 * D_MODEL

PEER_HEADS = 8
PEER_NKEYS = 128
PEER_EXPERTS = PEER_NKEYS * PEER_NKEYS
PEER_TOPK = 16
PEER_DQ = 256
PEER_SLOTS = PEER_HEADS * PEER_TOPK

LANES = 128
SUBLANES = 8
ROW_TILES = D_MODEL // LANES
PACKED_ROWS = ROW_TILES // 2
VMEM_LIMIT = 56 * 1024 * 1024

_NT = (((1,), (1,)), ((), ()))
_TN = (((0,), (0,)), ((), ()))


def _params(sem, vmem=None):
    return pltpu.CompilerParams(dimension_semantics=sem, vmem_limit_bytes=vmem)


def _resident(shape):
    nd = len(shape)
    return pl.BlockSpec(shape, lambda *_: (0,) * nd, pipeline_mode=pl.Buffered(1))


def _rms_mod(xf, g, shift, scale):
    y = xf * lax.rsqrt(jnp.mean(xf * xf, axis=-1, keepdims=True) + EPS)
    return (y * g) * (1.0 + scale) + shift


def _ada_body(c_ref, w_ref, b_ref, o_ref):
    c = c_ref[...]
    s = c * jax.nn.sigmoid(c)
    o_ref[...] = jnp.dot(s, w_ref[...], preferred_element_type=F32,
                         precision=lax.Precision.HIGHEST) + b_ref[...]


def ada_stage(cc, ada_w, ada_b, tn=512):
    r, d = cc.shape
    n = ada_w.shape[1]
    return pl.pallas_call(
        _ada_body,
        grid=(n // tn,),
        in_specs=[pl.BlockSpec((r, d), lambda j: (0, 0)),
                  pl.BlockSpec((d, tn), lambda j: (0, j)),
                  pl.BlockSpec((1, tn), lambda j: (0, j))],
        out_specs=pl.BlockSpec((r, tn), lambda j: (0, j)),
        out_shape=jax.ShapeDtypeStruct((r, n), F32),
        compiler_params=_params(("parallel",)),
        name="ada",
    )(cc, ada_w, ada_b.reshape(1, n))


def _ctx_body(lg_ref, ctx_ref, g_ref, sh_ref, sc_ref, w_ref, sf_ref, sb_ref):
    lc = ctx_ref.shape[0]
    hc = _rms_mod(ctx_ref[...], g_ref[...], sh_ref[...], sc_ref[...])
    kv = jnp.dot(hc.astype(BF16), w_ref[...], preferred_element_type=F32)
    m = lax.broadcasted_iota(I32, (lc, RET_DK), 0).astype(F32)
    for h in range(RET_HEADS):
        kf = kv[:, h * RET_DK:(h + 1) * RET_DK] * K_SCALE
        vb = kv[:, QK_WIDTH + h * RET_DV:QK_WIDTH + (h + 1) * RET_DV].astype(BF16)
        wf = jnp.exp(lg_ref[0, h] * (lc - 1.0 - m))
        wb = jnp.exp(lg_ref[1, h] * m)
        sf_ref[h] = lax.dot_general((kf * wf).astype(BF16), vb, _TN, preferred_element_type=F32)
        sb_ref[h] = lax.dot_general((kf * wb).astype(BF16), vb, _TN, preferred_element_type=F32)


def ctx_stage(lg, ctx, norm_g, csh, csc, w_kv):
    b, lc, d = ctx.shape
    vec = pl.BlockSpec((1, d), lambda i: (0, 0))
    st = jax.ShapeDtypeStruct((b, RET_HEADS, RET_DK, RET_DV), F32)
    st_spec = pl.BlockSpec((None, RET_HEADS, RET_DK, RET_DV), lambda i: (i, 0, 0, 0))
    return pl.pallas_call(
        _ctx_body,
        grid=(b,),
        in_specs=[pl.BlockSpec(memory_space=pltpu.SMEM),
                  pl.BlockSpec((None, lc, d), lambda i: (i, 0, 0)),
                  vec, vec, vec,
                  pl.BlockSpec(w_kv.shape, lambda i: (0, 0))],
        out_specs=[st_spec, st_spec],
        out_shape=[st, st],
        compiler_params=_params(("parallel",)),
        name="ctx",
    )(lg, ctx, norm_g.reshape(1, d), csh.reshape(1, d), csc.reshape(1, d), w_kv)


def _rope(a, cos, sin_signed):
    return a * cos + pltpu.roll(a, RET_DK // 2, 1) * sin_signed


def _proj_body(x_ref, g_ref, sh_ref, sc_ref, cos_ref, sin_ref, w_ref,
               p_ref, q_ref, k_ref, v_ref, gz_ref, mg_ref):
    h = _rms_mod(x_ref[...], g_ref[...], sh_ref[...], sc_ref[...]).astype(BF16)

    def mm(lo, hi):
        return jnp.dot(h, w_ref[:, lo:hi], preferred_element_type=F32)

    p_ref[...] = mm(OFF_POOL, OFF_Q)
    cos = cos_ref[...]
    sin = sin_ref[...]
    qf = mm(OFF_Q, OFF_K)
    kf = mm(OFF_K, OFF_V)
    for hd in range(RET_HEADS):
        sl = slice(hd * RET_DK, (hd + 1) * RET_DK)
        q_ref[:, sl] = _rope(qf[:, sl], cos, sin).astype(BF16)
        k_ref[:, sl] = (_rope(kf[:, sl], cos, sin) * K_SCALE).astype(BF16)
    v_ref[...] = mm(OFF_V, OFF_G).astype(BF16)
    gz_ref[...] = mm(OFF_G, OFF_MERGE).astype(BF16)
    mg_ref[...] = mm(OFF_MERGE, IN_WIDTH).astype(BF16)


def proj_stage(x, norm_g, sh1, sc1, cos, sin, w_in_bf, tm=512):
    b, l, d = x.shape
    vec_b = pl.BlockSpec((None, 1, d), lambda i, j: (i, 0, 0))
    rows = lambda w: pl.BlockSpec((None, tm, w), lambda i, j: (i, j, 0))
    tab = pl.BlockSpec((tm, RET_DK), lambda i, j: (j, 0))
    outs = [(POOL_WIDTH, F32), (QK_WIDTH, BF16), (QK_WIDTH, BF16), (V_WIDTH, BF16),
            (V_WIDTH, BF16), (2 * D_MODEL, BF16)]
    return pl.pallas_call(
        _proj_body,
        grid=(b, l // tm),
        in_specs=[rows(d), pl.BlockSpec((1, d), lambda i, j: (0, 0)), vec_b, vec_b, tab, tab,
                  _resident(w_in_bf.shape)],
        out_specs=[rows(w) for w, _ in outs],
        out_shape=[jax.ShapeDtypeStruct((b, l, w), dt) for w, dt in outs],
        compiler_params=_params(("parallel", "parallel"), VMEM_LIMIT),
        name="proj",
    )(x, norm_g.reshape(1, d), sh1, sc1, cos, sin, w_in_bf)


def _ret_body(lg_ref, q_ref, k_ref, v_ref, sf_ref, sb_ref, y_ref, s_scr):
    c = RET_CHUNK
    n_chunks = q_ref.shape[0] // c
    hd = pl.program_id(1)
    lgf = lg_ref[0, hd]
    lgb = lg_ref[1, hd]
    n_i = lax.broadcasted_iota(I32, (c, c), 0)
    m_i = lax.broadcasted_iota(I32, (c, c), 1)
    rel = (n_i - m_i).astype(F32)
    intra_f = jnp.where(rel >= 0, jnp.exp(lgf * jnp.where(rel >= 0, rel, 0.0)), 0.0)
    intra_b = jnp.where(rel < 0, jnp.exp(lgb * jnp.where(rel < 0, -rel, 0.0)), 0.0)
    pos = lax.broadcasted_iota(I32, (c, RET_DK), 0).astype(F32)
    qdec_f = jnp.exp(lgf * (pos + 1.0))
    kdec_f = jnp.exp(lgf * (c - 1.0 - pos))
    qdec_b = jnp.exp(lgb * (c - pos))
    kdec_b = jnp.exp(lgb * pos)
    blk_f = jnp.exp(jnp.full((1, RET_DV), lgf * c, F32))
    blk_b = jnp.exp(jnp.full((1, RET_DV), lgb * c, F32))

    def chunk(i, intra, qdec, kdec, blk):
        rows = pl.ds(pl.multiple_of(i * c, c), c)
        qi = q_ref[rows, :]
        ki = k_ref[rows, :]
        vi = v_ref[rows, :]
        sc = lax.dot_general(qi, ki, _NT, preferred_element_type=F32) * intra
        s = s_scr[...]
        y = (jnp.dot(sc.astype(BF16), vi, preferred_element_type=F32)
             + jnp.dot((qi.astype(F32) * qdec).astype(BF16), s.astype(BF16),
                       preferred_element_type=F32))
        s_scr[...] = s * blk + lax.dot_general((ki.astype(F32) * kdec).astype(BF16), vi, _TN,
                                               preferred_element_type=F32)
        return rows, y

    s_scr[...] = sf_ref[...]

    def fwd(i, carry):
        rows, y = chunk(i, intra_f, qdec_f, kdec_f, blk_f)
        y_ref[rows, :] = y
        return carry

    lax.fori_loop(0, n_chunks, fwd, 0)
    s_scr[...] = sb_ref[...]

    def bwd(j, carry):
        rows, y = chunk(n_chunks - 1 - j, intra_b, qdec_b, kdec_b, blk_b)
        y_ref[rows, :] += y
        return carry

    lax.fori_loop(0, n_chunks, bwd, 0)


def ret_stage(lg, q, k, v, s_f, s_b):
    b, l, _ = q.shape
    qk_spec = pl.BlockSpec((None, l, RET_DK), lambda i, j: (i, 0, j))
    v_spec = pl.BlockSpec((None, l, RET_DV), lambda i, j: (i, 0, j))
    st_spec = pl.BlockSpec((None, None, RET_DK, RET_DV), lambda i, j: (i, j, 0, 0))
    return pl.pallas_call(
        _ret_body,
        grid=(b, RET_HEADS),
        in_specs=[pl.BlockSpec(memory_space=pltpu.SMEM), qk_spec, qk_spec, v_spec, st_spec, st_spec],
        out_specs=v_spec,
        out_shape=jax.ShapeDtypeStruct((b, l, V_WIDTH), F32),
        scratch_shapes=[pltpu.VMEM((RET_DK, RET_DV), F32)],
        compiler_params=_params(("parallel", "parallel")),
        name="ret",
    )(lg, q, k, v, s_f, s_b)


def _pool_bands(tm):
    r = np.arange(tm)[:, None]
    c = np.arange(tm)[None, :]
    bands = np.zeros((len(POOL_WINDOWS), 3, tm, tm), np.float32)
    for gi, w in enumerate(POOL_WINDOWS):
        lo, hi = r - w // 2, r + w - w // 2
        for j, off in enumerate((-tm, 0, tm)):
            bands[gi, j] = ((c + off >= lo) & (c + off < hi)).astype(np.float32)
    return jnp.asarray(bands, BF16)


def _mix_body(pp_ref, pm_ref, pn_ref, y_ref, gz_ref, mg_ref, x_ref, g1_ref, sh2_ref, sc2_ref,
              band_ref, pw_ref, ps_ref, po_ref, rg_ref, ro_ref, wo_ref, ng_ref,
              x1_ref, hf_ref, *, seq_len):
    tm = pm_ref.shape[0]
    li = pl.program_id(1)
    has_prev = (li > 0).astype(F32)
    has_next = (li < pl.num_programs(1) - 1).astype(F32)
    t = li * tm + lax.broadcasted_iota(I32, (tm, POOL_GROUP), 0)

    def window_sum(ref, cols, gi, j):
        pf = ref[:, cols]
        hi = pf.astype(BF16)
        lo = (pf - hi.astype(F32)).astype(BF16)
        band = band_ref[gi, j]
        return (jnp.dot(band, hi, preferred_element_type=F32)
                + jnp.dot(band, lo, preferred_element_type=F32))

    mixed = []
    for gi, w in enumerate(POOL_WINDOWS):
        cols = slice(gi * POOL_GROUP, (gi + 1) * POOL_GROUP)
        ws = (window_sum(pm_ref, cols, gi, 1) + has_prev * window_sum(pp_ref, cols, gi, 0)
              + has_next * window_sum(pn_ref, cols, gi, 2))
        cnt = (jnp.clip(t + (w - w // 2), 0, seq_len) - jnp.clip(t - w // 2, 0, seq_len)).astype(F32)
        dgi = ws / cnt - pm_ref[:, cols]
        mixed.append(jnp.dot(dgi.astype(BF16), pw_ref[gi], preferred_element_type=F32))
    mixed = jnp.concatenate(mixed, axis=1) * ps_ref[...]
    pool = jnp.dot(mixed.astype(BF16), po_ref[...], preferred_element_type=F32)

    yn = []
    for hd in range(RET_HEADS):
        yh = y_ref[:, hd * RET_DV:(hd + 1) * RET_DV]
        mu = jnp.mean(yh, axis=-1, keepdims=True)
        yc = yh - mu
        var = jnp.mean(yc * yc, axis=-1, keepdims=True)
        yn.append(yc * lax.rsqrt(var + EPS))
    yn = jnp.concatenate(yn, axis=1) * rg_ref[...]
    gate = gz_ref[...].astype(F32)
    ret = jnp.dot((yn * (gate * jax.nn.sigmoid(gate))).astype(BF16), ro_ref[...],
                  preferred_element_type=F32)

    g_pool = mg_ref[:, :D_MODEL].astype(F32)
    g_ret = mg_ref[:, D_MODEL:].astype(F32)
    merged = jax.nn.sigmoid(g_pool) * pool + jax.nn.sigmoid(g_ret) * ret
    out = jnp.dot(merged.astype(BF16), wo_ref[...], preferred_element_type=F32)
    x1 = x_ref[...] + g1_ref[...] * out
    x1_ref[...] = x1
    hf_ref[...] = _rms_mod(x1, ng_ref[...], sh2_ref[...], sc2_ref[...])


def mix_stage(p, y, gz, mg, x, g1, sh2, sc2, pool_w_bf, pool_scale, pool_out_bf, ret_norm_g,
              ret_out_bf, w_out_bf, norm_ffn_g, tm=256):
    b, l, d = x.shape
    nl = l // tm
    bands = _pool_bands(tm)
    rows = lambda w: pl.BlockSpec((None, tm, w), lambda i, j: (i, j, 0))
    vec_b = pl.BlockSpec((None, 1, d), lambda i, j: (i, 0, 0))
    const = lambda a: pl.BlockSpec(a.shape, lambda i, j: (0,) * a.ndim)
    ps = pool_scale.reshape(1, POOL_WIDTH)
    rg = ret_norm_g.reshape(1, V_WIDTH)
    ng = norm_ffn_g.reshape(1, d)
    return pl.pallas_call(
        functools.partial(_mix_body, seq_len=l),
        grid=(b, nl),
        in_specs=[pl.BlockSpec((None, tm, POOL_WIDTH), lambda i, j: (i, jnp.maximum(j - 1, 0), 0)),
                  rows(POOL_WIDTH),
                  pl.BlockSpec((None, tm, POOL_WIDTH), lambda i, j: (i, jnp.minimum(j + 1, nl - 1), 0)),
                  rows(V_WIDTH), rows(V_WIDTH), rows(2 * D_MODEL), rows(d), vec_b, vec_b, vec_b,
                  const(bands), const(pool_w_bf), const(ps), const(pool_out_bf), const(rg),
                  const(ret_out_bf), const(w_out_bf), const(ng)],
        out_specs=[rows(d), rows(d)],
        out_shape=[jax.ShapeDtypeStruct((b, l, d), F32), jax.ShapeDtypeStruct((b, l, d), F32)],
        compiler_params=_params(("parallel", "parallel"), VMEM_LIMIT),
        name="mix",
    )(p, p, p, y, gz, mg, x, g1, sh2, sc2, bands, pool_w_bf, ps, pool_out_bf, rg, ret_out_bf,
      w_out_bf, ng)


def _topk_rows(s, k):
    r, n = s.shape
    rows = lax.broadcasted_iota(I32, (r, n), 0)
    slot = lax.broadcasted_iota(I32, (k, n), 0)
    vals = jnp.zeros((k, n), F32)
    idxs = jnp.zeros((k, n), I32)
    for j in range(k):
        m = jnp.max(s, axis=0, keepdims=True)
        i = jnp.min(jnp.where(s == m, rows, r), axis=0, keepdims=True)
        vals = jnp.where(slot == j, m, vals)
        idxs = jnp.where(slot == j, i, idxs)
        s = jnp.where(rows == i, -jnp.inf, s)
    return vals, idxs


def _pick_rows(table, sel):
    out = jnp.zeros(sel.shape, table.dtype)
    for r in range(table.shape[0]):
        out = jnp.where(sel == r, table[r:r + 1, :], out)
    return out


def _route_body(hf_ref, wq_ref, keys_ref, e_ref, g_ref):
    tq = hf_ref.shape[0]
    half = PEER_DQ // 2
    q = jnp.dot(hf_ref[...].astype(BF16), wq_ref[...], preferred_element_type=F32).astype(BF16)
    for cb in range(tq // LANES):
        tok = slice(cb * LANES, (cb + 1) * LANES)
        e_heads, g_heads = [], []
        for hd in range(PEER_HEADS):
            sub = []
            for part in range(2):
                col = (hd * 2 + part) * half
                st = lax.dot_general(keys_ref[hd, part], q[tok, col:col + half], _NT,
                                     preferred_element_type=F32)
                sub.append(_topk_rows(st, PEER_TOPK))
            (s1, i1), (s2, i2) = sub
            cand = jnp.concatenate([s1[i:i + 1, :] + s2 for i in range(PEER_TOPK)], axis=0)
            best, pos = _topk_rows(cand, PEER_TOPK)
            e1 = _pick_rows(i1, pos >> 4)
            e2 = _pick_rows(i2, pos & (PEER_TOPK - 1))
            ex = jnp.exp(best - best[0:1, :])
            e_heads.append((e1 * PEER_NKEYS + e2) * PACKED_ROWS)
            g_heads.append(ex / jnp.sum(ex, axis=0, keepdims=True))
        e_ref[tok, :] = jnp.concatenate(e_heads, axis=0).T
        g_ref[tok, :] = jnp.concatenate(g_heads, axis=0).T


def route_stage(hf, wq_bf, keys_bf, tq=256):
    n, d = hf.shape
    rows = lambda w: pl.BlockSpec((tq, w), lambda i: (i, 0))
    return pl.pallas_call(
        _route_body,
        grid=(n // tq,),
        in_specs=[rows(d), pl.BlockSpec(wq_bf.shape, lambda i: (0, 0)),
                  pl.BlockSpec(keys_bf.shape, lambda i: (0, 0, 0, 0))],
        out_specs=[rows(PEER_SLOTS), rows(PEER_SLOTS)],
        out_shape=[jax.ShapeDtypeStruct((n, PEER_SLOTS), I32),
                   jax.ShapeDtypeStruct((n, PEER_SLOTS), F32)],
        compiler_params=_params(("parallel",), VMEM_LIMIT),
        name="route",
    )(hf, wq_bf, keys_bf)


def _pack_table(t):
    tb = t.astype(BF16).reshape(PEER_EXPERTS, PACKED_ROWS, 2, LANES).transpose(0, 1, 3, 2)
    return lax.bitcast_convert_type(tb, jnp.uint32).reshape(PEER_EXPERTS * PACKED_ROWS, LANES)


def _table_tile(words):
    return pltpu.bitcast(words, BF16).astype(F32)


OFF_BITS = 16
assert PEER_EXPERTS * PACKED_ROWS <= 1 << OFF_BITS
_WINDOW = 8


def _token_rows(off_ref, tbl_ref, base):
    offs = []
    for j in range(PEER_SLOTS // _WINDOW):
        wbase = base + j * _WINDOW
        if j >= 2:
            wbase = wbase + (offs[(j - 1) * _WINDOW - 1] >> OFF_BITS)
        offs += [off_ref[wbase + i] for i in range(_WINDOW)]
    return [tbl_ref[pl.ds(pl.multiple_of(o, PACKED_ROWS), PACKED_ROWS), :] for o in offs]


def _peer_u_body(off_ref, h_ref, g_ref, tbl_ref, o_ref, p_scr, s_scr):
    tb = h_ref.shape[0]
    octet = SUBLANES * SUBLANES

    def token(t, carry):
        hv = h_ref[t]
        for k, words in enumerate(_token_rows(off_ref, tbl_ref, t * PEER_SLOTS)):
            p_scr[pl.ds(k * SUBLANES, SUBLANES), :] = _table_tile(words) * hv
        for j in range(PEER_SLOTS // SUBLANES):
            acc = p_scr[pl.ds(j * octet, SUBLANES, stride=SUBLANES), :]
            for s in range(1, SUBLANES):
                acc = acc + p_scr[pl.ds(j * octet + s, SUBLANES, stride=SUBLANES), :]
            s_scr[pl.ds(pl.multiple_of(t * PEER_SLOTS + j * SUBLANES, SUBLANES), SUBLANES), :] = acc
        return carry

    lax.fori_loop(0, tb, token, 0)

    eye = (lax.broadcasted_iota(I32, (PEER_SLOTS, LANES), 0)
           == lax.broadcasted_iota(I32, (PEER_SLOTS, LANES), 1))

    def finish(c, carry):
        r0 = pl.multiple_of(c * SUBLANES, SUBLANES)
        acts = []
        for i in range(SUBLANES):
            part = s_scr[pl.ds(pl.multiple_of((r0 + i) * PEER_SLOTS, PEER_SLOTS), PEER_SLOTS), :]
            tot = jnp.sum(part, axis=1, keepdims=True)
            acts.append(jnp.sum(jnp.where(eye, tot, 0.0), axis=0, keepdims=True))
        act = jnp.concatenate(acts, axis=0)
        rows8 = pl.ds(r0, SUBLANES)
        o_ref[rows8, :] = g_ref[rows8, :] * (0.5 * act * (1.0 + lax.erf(act * (2.0 ** -0.5))))
        return carry

    lax.fori_loop(0, tb // SUBLANES, finish, 0)


def _smem_slots(tb):
    return pl.BlockSpec((tb * PEER_SLOTS,), lambda i: (i,), memory_space=pltpu.SMEM)


def peer_u_stage(eoff, hf3, gates, tbl, tb=128):
    n = eoff.shape[0]
    rows = lambda: pl.BlockSpec((tb, PEER_SLOTS), lambda i: (i, 0))
    return pl.pallas_call(
        _peer_u_body,
        grid=(n // tb,),
        in_specs=[_smem_slots(tb), pl.BlockSpec((tb, ROW_TILES, LANES), lambda i: (i, 0, 0)), rows(),
                  _resident(tbl.shape)],
        out_specs=rows(),
        out_shape=jax.ShapeDtypeStruct((n, PEER_SLOTS), F32),
        scratch_shapes=[pltpu.VMEM((PEER_SLOTS * SUBLANES, LANES), F32),
                        pltpu.VMEM((tb * PEER_SLOTS, LANES), F32)],
        compiler_params=_params(("arbitrary",), VMEM_LIMIT),
        name="peer_u",
    )(eoff.reshape(-1), hf3, gates, tbl)


_V_ACCS = 4


def _peer_v_body(off_ref, w_ref, x_ref, g2_ref, fg_ref, tbl_ref, o_ref):
    tb = x_ref.shape[0]

    def token(t, carry):
        wrep = jnp.broadcast_to(w_ref[t], (LANES, PEER_SLOTS)).T
        accs = [None] * _V_ACCS
        for k, words in enumerate(_token_rows(off_ref, tbl_ref, t * PEER_SLOTS)):
            term = jnp.broadcast_to(wrep[k:k + 1, :], (ROW_TILES, LANES)) * _table_tile(words)
            accs[k % _V_ACCS] = term if accs[k % _V_ACCS] is None else accs[k % _V_ACCS] + term
        o_ref[t] = (accs[0] + accs[1]) + (accs[2] + accs[3])
        return carry

    lax.fori_loop(0, tb, token, 0)
    x2 = x_ref[...] + g2_ref[...] * o_ref[...]
    ms = jnp.sum(jnp.sum(x2 * x2, axis=2, keepdims=True), axis=1, keepdims=True) / D_MODEL
    o_ref[...] = (x2 * lax.rsqrt(ms + EPS)) * fg_ref[...]


def peer_v_stage(eoff, w, x3, g2, final_g, tbl, tokens_per_batch, tb=128):
    n = eoff.shape[0]
    per_b = tokens_per_batch // tb
    tile = lambda: pl.BlockSpec((tb, ROW_TILES, LANES), lambda i: (i, 0, 0))
    return pl.pallas_call(
        _peer_v_body,
        grid=(n // tb,),
        in_specs=[_smem_slots(tb), pl.BlockSpec((tb, 1, PEER_SLOTS), lambda i: (i, 0, 0)), tile(),
                  pl.BlockSpec((None, ROW_TILES, LANES), lambda i: (i // per_b, 0, 0)),
                  pl.BlockSpec((ROW_TILES, LANES), lambda i: (0, 0)),
                  _resident(tbl.shape)],
        out_specs=tile(),
        out_shape=jax.ShapeDtypeStruct((n, ROW_TILES, LANES), F32),
        compiler_params=_params(("arbitrary",), VMEM_LIMIT),
        name="peer_v",
    )(eoff.reshape(-1), w.reshape(n, 1, PEER_SLOTS), x3, g2, final_g.reshape(ROW_TILES, LANES), tbl)


def _rope_tables(l):
    quarter = RET_DK // 4
    rows = l // GRID_W
    row = jnp.repeat(jnp.arange(rows, dtype=F32), GRID_W)
    col = jnp.tile(jnp.arange(GRID_W, dtype=F32), rows)
    inv = ROPE_BASE ** (-jnp.arange(quarter, dtype=F32) / quarter)
    ang = jnp.concatenate([row[:, None] * inv, col[:, None] * inv], axis=-1)
    cos, sin = jnp.cos(ang), jnp.sin(ang)
    return jnp.concatenate([cos, cos], axis=-1), jnp.concatenate([-sin, sin], axis=-1)


def _layer(x, ctx, c, c_ctx, ada_w, ada_b, norm_mix_g, norm_ffn_g, w_in, pool_w, pool_scale,
           pool_out, ret_decay, ret_norm_g, ret_out, w_out, peer_wq, peer_keys, peer_u, peer_v,
           final_g):
    b, l, d = x.shape
    n = b * l

    rows = -(-(b + 1) // SUBLANES) * SUBLANES
    cc = jnp.zeros((rows, d), F32).at[:b].set(c).at[b].set(c_ctx)
    mod = ada_stage(cc, ada_w, ada_b)
    sh1, sc1, g1, sh2, sc2, g2 = [m.reshape(b, 1, d) for m in jnp.split(mod[:b], 6, axis=-1)]
    csh1, csc1 = mod[b, :d], mod[b, d:2 * d]
    lg = jax.nn.log_sigmoid(ret_decay.astype(F32))

    w_in_bf = w_in.astype(BF16)
    s_f, s_b = ctx_stage(lg, ctx, norm_mix_g, csh1, csc1, w_in_bf[:, OFF_K:OFF_G])

    cos, sin = _rope_tables(l)
    p, q, k, v, gz, mg = proj_stage(x, norm_mix_g, sh1, sc1, cos, sin, w_in_bf)
    y = ret_stage(lg, q, k, v, s_f, s_b)
    x1, hf = mix_stage(p, y, gz, mg, x, g1, sh2, sc2, pool_w.astype(BF16), pool_scale,
                       pool_out.astype(BF16), ret_norm_g, ret_out.astype(BF16),
                       w_out.astype(BF16), norm_ffn_g)

    hf2 = hf.reshape(n, d)
    eoff, gates = route_stage(hf2, peer_wq.astype(BF16), peer_keys.astype(BF16))
    w = peer_u_stage(eoff, hf2.reshape(n, ROW_TILES, LANES), gates, _pack_table(peer_u))
    out = peer_v_stage(eoff, w, x1.reshape(n, ROW_TILES, LANES), g2.reshape(b, ROW_TILES, LANES),
                       final_g, _pack_table(peer_v), l)
    return out.reshape(b, l, d)


def kernel(x, c, ctx, c_ctx, ada_w, ada_b, norm_mix_g, norm_ffn_g, w_in, pool_w, pool_scale, pool_out, ret_decay, ret_norm_g, ret_out, w_out, peer_wq, peer_keys, peer_u, peer_v, final_g):
    assert ada_w.shape[0] == 1, "single-layer block"
    return _layer(x, ctx, c, c_ctx, ada_w[0], ada_b[0], norm_mix_g[0], norm_ffn_g[0], w_in[0],
                  pool_w[0], pool_scale[0], pool_out[0], ret_decay[0], ret_norm_g[0], ret_out[0],
                  w_out[0], peer_wq[0], peer_keys[0], peer_u[0], peer_v[0], final_g)
```

```python
import functools

import jax
import jax.numpy as jnp
import numpy as np
from jax import lax
from jax.experimental import pallas as pl
from jax.experimental.pallas import tpu as pltpu

F32 = jnp.float32
BF16 = jnp.bfloat16
I32 = jnp.int32

D_MODEL = 1024
GRID_W = 64
EPS = 1e-6

POOL_WINDOWS = (2, 4, 8, 16)
POOL_WIDTH = D_MODEL // 2
POOL_GROUP = POOL_WIDTH // len(POOL_WINDOWS)

RET_HEADS = 4
RET_DK = 128
RET_DV = 256
RET_CHUNK = 128
ROPE_BASE = 10000.0
QK_WIDTH = RET_HEADS * RET_DK
V_WIDTH = RET_HEADS * RET_DV
K_SCALE = RET_DK ** -0.5

OFF_POOL = 0
OFF_Q = OFF_POOL + POOL_WIDTH
OFF_K = OFF_Q + QK_WIDTH
OFF_V = OFF_K + QK_WIDTH
OFF_G = OFF_V + V_WIDTH
OFF_MERGE = OFF_G + V_WIDTH
IN_WIDTH = OFF_MERGE + 2 * D_MODEL

PEER_HEADS = 8
PEER_NKEYS = 128
PEER_EXPERTS = PEER_NKEYS * PEER_NKEYS
PEER_TOPK = 16
PEER_DQ = 256
PEER_SLOTS = PEER_HEADS * PEER_TOPK

LANES = 128
SUBLANES = 8
ROW_TILES = D_MODEL // LANES
PACKED_ROWS = ROW_TILES // 2
VMEM_LIMIT = 56 * 1024 * 1024

_NT = (((1,), (1,)), ((), ()))
_TN = (((0,), (0,)), ((), ()))


def _params(sem, vmem=None):
    return pltpu.CompilerParams(dimension_semantics=sem, vmem_limit_bytes=vmem)


def _resident(shape):
    nd = len(shape)
    return pl.BlockSpec(shape, lambda *_: (0,) * nd, pipeline_mode=pl.Buffered(1))


def _rms_mod(xf, g, shift, scale):
    y = xf * lax.rsqrt(jnp.mean(xf * xf, axis=-1, keepdims=True) + EPS)
    return (y * g) * (1.0 + scale) + shift


def _ada_body(c_ref, w_ref, b_ref, o_ref):
    c = c_ref[...]
    s = c * jax.nn.sigmoid(c)
    o_ref[...] = jnp.dot(s, w_ref[...], preferred_element_type=F32,
                         precision=lax.Precision.HIGHEST) + b_ref[...]


def ada_stage(cc, ada_w, ada_b, tn=512):
    r, d = cc.shape
    n = ada_w.shape[1]
    return pl.pallas_call(
        _ada_body,
        grid=(n // tn,),
        in_specs=[pl.BlockSpec((r, d), lambda j: (0, 0)),
                  pl.BlockSpec((d, tn), lambda j: (0, j)),
                  pl.BlockSpec((1, tn), lambda j: (0, j))],
        out_specs=pl.BlockSpec((r, tn), lambda j: (0, j)),
        out_shape=jax.ShapeDtypeStruct((r, n), F32),
        compiler_params=_params(("parallel",)),
        name="ada",
    )(cc, ada_w, ada_b.reshape(1, n))


def _ctx_body(lg_ref, ctx_ref, g_ref, sh_ref, sc_ref, w_ref, sf_ref, sb_ref):
    lc = ctx_ref.shape[0]
    hc = _rms_mod(ctx_ref[...], g_ref[...], sh_ref[...], sc_ref[...])
    kv = jnp.dot(hc.astype(BF16), w_ref[...], preferred_element_type=F32)
    m = lax.broadcasted_iota(I32, (lc, RET_DK), 0).astype(F32)
    for h in range(RET_HEADS):
        kf = kv[:, h * RET_DK:(h + 1) * RET_DK] * K_SCALE
        vb = kv[:, QK_WIDTH + h * RET_DV:QK_WIDTH + (h + 1) * RET_DV].astype(BF16)
        wf = jnp.exp(lg_ref[0, h] * (lc - 1.0 - m))
        wb = jnp.exp(lg_ref[1, h] * m)
        sf_ref[h] = lax.dot_general((kf * wf).astype(BF16), vb, _TN, preferred_element_type=F32)
        sb_ref[h] = lax.dot_general((kf * wb).astype(BF16), vb, _TN, preferred_element_type=F32)


def ctx_stage(lg, ctx, norm_g, csh, csc, w_kv):
    b, lc, d = ctx.shape
    vec = pl.BlockSpec((1, d), lambda i: (0, 0))
    st = jax.ShapeDtypeStruct((b, RET_HEADS, RET_DK, RET_DV), F32)
    st_spec = pl.BlockSpec((None, RET_HEADS, RET_DK, RET_DV), lambda i: (i, 0, 0, 0))
    return pl.pallas_call(
        _ctx_body,
        grid=(b,),
        in_specs=[pl.BlockSpec(memory_space=pltpu.SMEM),
                  pl.BlockSpec((None, lc, d), lambda i: (i, 0, 0)),
                  vec, vec, vec,
                  pl.BlockSpec(w_kv.shape, lambda i: (0, 0))],
        out_specs=[st_spec, st_spec],
        out_shape=[st, st],
        compiler_params=_params(("parallel",)),
        name="ctx",
    )(lg, ctx, norm_g.reshape(1, d), csh.reshape(1, d), csc.reshape(1, d), w_kv)


def _rope(a, cos, sin_signed):
    return a * cos + pltpu.roll(a, RET_DK // 2, 1) * sin_signed


def _proj_body(x_ref, g_ref, sh_ref, sc_ref, cos_ref, sin_ref, w_ref,
               p_ref, q_ref, k_ref, v_ref, gz_ref, mg_ref):
    h = _rms_mod(x_ref[...], g_ref[...], sh_ref[...], sc_ref[...]).astype(BF16)

    def mm(lo, hi):
        return jnp.dot(h, w_ref[:, lo:hi], preferred_element_type=F32)

    p_ref[...] = mm(OFF_POOL, OFF_Q)
    cos = cos_ref[...]
    sin = sin_ref[...]
    qf = mm(OFF_Q, OFF_K)
    kf = mm(OFF_K, OFF_V)
    for hd in range(RET_HEADS):
        sl = slice(hd * RET_DK, (hd + 1) * RET_DK)
        q_ref[:, sl] = _rope(qf[:, sl], cos, sin).astype(BF16)
        k_ref[:, sl] = (_rope(kf[:, sl], cos, sin) * K_SCALE).astype(BF16)
    v_ref[...] = mm(OFF_V, OFF_G).astype(BF16)
    gz_ref[...] = mm(OFF_G, OFF_MERGE).astype(BF16)
    mg_ref[...] = mm(OFF_MERGE, IN_WIDTH).astype(BF16)


def proj_stage(x, norm_g, sh1, sc1, cos, sin, w_in_bf, tm=512):
    b, l, d = x.shape
    vec_b = pl.BlockSpec((None, 1, d), lambda i, j: (i, 0, 0))
    rows = lambda w: pl.BlockSpec((None, tm, w), lambda i, j: (i, j, 0))
    tab = pl.BlockSpec((tm, RET_DK), lambda i, j: (j, 0))
    outs = [(POOL_WIDTH, F32), (QK_WIDTH, BF16), (QK_WIDTH, BF16), (V_WIDTH, BF16),
            (V_WIDTH, BF16), (2 * D_MODEL, BF16)]
    return pl.pallas_call(
        _proj_body,
        grid=(b, l // tm),
        in_specs=[rows(d), pl.BlockSpec((1, d), lambda i, j: (0, 0)), vec_b, vec_b, tab, tab,
                  _resident(w_in_bf.shape)],
        out_specs=[rows(w) for w, _ in outs],
        out_shape=[jax.ShapeDtypeStruct((b, l, w), dt) for w, dt in outs],
        compiler_params=_params(("parallel", "parallel"), VMEM_LIMIT),
        name="proj",
    )(x, norm_g.reshape(1, d), sh1, sc1, cos, sin, w_in_bf)


def _ret_body(lg_ref, q_ref, k_ref, v_ref, sf_ref, sb_ref, y_ref, s_scr):
    c = RET_CHUNK
    n_chunks = q_ref.shape[0] // c
    hd = pl.program_id(1)
    lgf = lg_ref[0, hd]
    lgb = lg_ref[1, hd]
    n_i = lax.broadcasted_iota(I32, (c, c), 0)
    m_i = lax.broadcasted_iota(I32, (c, c), 1)
    rel = (n_i - m_i).astype(F32)
    intra_f = jnp.where(rel >= 0, jnp.exp(lgf * jnp.where(rel >= 0, rel, 0.0)), 0.0)
    intra_b = jnp.where(rel < 0, jnp.exp(lgb * jnp.where(rel < 0, -rel, 0.0)), 0.0)
    pos = lax.broadcasted_iota(I32, (c, RET_DK), 0).astype(F32)
    qdec_f = jnp.exp(lgf * (pos + 1.0))
    kdec_f = jnp.exp(lgf * (c - 1.0 - pos))
    qdec_b = jnp.exp(lgb * (c - pos))
    kdec_b = jnp.exp(lgb * pos)
    blk_f = jnp.exp(jnp.full((1, RET_DV), lgf * c, F32))
    blk_b = jnp.exp(jnp.full((1, RET_DV), lgb * c, F32))

    def chunk(i, intra, qdec, kdec, blk):
        rows = pl.ds(pl.multiple_of(i * c, c), c)
        qi = q_ref[rows, :]
        ki = k_ref[rows, :]
        vi = v_ref[rows, :]
        sc = lax.dot_general(qi, ki, _NT, preferred_element_type=F32) * intra
        s = s_scr[...]
        y = (jnp.dot(sc.astype(BF16), vi, preferred_element_type=F32)
             + jnp.dot((qi.astype(F32) * qdec).astype(BF16), s.astype(BF16),
                       preferred_element_type=F32))
        s_scr[...] = s * blk + lax.dot_general((ki.astype(F32) * kdec).astype(BF16), vi, _TN,
                                               preferred_element_type=F32)
        return rows, y

    s_scr[...] = sf_ref[...]

    def fwd(i, carry):
        rows, y = chunk(i, intra_f, qdec_f, kdec_f, blk_f)
        y_ref[rows, :] = y
        return carry

    lax.fori_loop(0, n_chunks, fwd, 0)
    s_scr[...] = sb_ref[...]

    def bwd(j, carry):
        rows, y = chunk(n_chunks - 1 - j, intra_b, qdec_b, kdec_b, blk_b)
        y_ref[rows, :] += y
        return carry

    lax.fori_loop(0, n_chunks, bwd, 0)


def ret_stage(lg, q, k, v, s_f, s_b):
    b, l, _ = q.shape
    qk_spec = pl.BlockSpec((None, l, RET_DK), lambda i, j: (i, 0, j))
    v_spec = pl.BlockSpec((None, l, RET_DV), lambda i, j: (i, 0, j))
    st_spec = pl.BlockSpec((None, None, RET_DK, RET_DV), lambda i, j: (i, j, 0, 0))
    return pl.pallas_call(
        _ret_body,
        grid=(b, RET_HEADS),
        in_specs=[pl.BlockSpec(memory_space=pltpu.SMEM), qk_spec, qk_spec, v_spec, st_spec, st_spec],
        out_specs=v_spec,
        out_shape=jax.ShapeDtypeStruct((b, l, V_WIDTH), F32),
        scratch_shapes=[pltpu.VMEM((RET_DK, RET_DV), F32)],
        compiler_params=_params(("parallel", "parallel")),
        name="ret",
    )(lg, q, k, v, s_f, s_b)


def _pool_bands(tm):
    r = np.arange(tm)[:, None]
    c = np.arange(tm)[None, :]
    bands = np.zeros((len(POOL_WINDOWS), 3, tm, tm), np.float32)
    for gi, w in enumerate(POOL_WINDOWS):
        lo, hi = r - w // 2, r + w - w // 2
        for j, off in enumerate((-tm, 0, tm)):
            bands[gi, j] = ((c + off >= lo) & (c + off < hi)).astype(np.float32)
    return jnp.asarray(bands, BF16)


def _mix_body(pp_ref, pm_ref, pn_ref, y_ref, gz_ref, mg_ref, x_ref, g1_ref, sh2_ref, sc2_ref,
              band_ref, pw_ref, ps_ref, po_ref, rg_ref, ro_ref, wo_ref, ng_ref,
              x1_ref, hf_ref, *, seq_len):
    tm = pm_ref.shape[0]
    li = pl.program_id(1)
    has_prev = (li > 0).astype(F32)
    has_next = (li < pl.num_programs(1) - 1).astype(F32)
    t = li * tm + lax.broadcasted_iota(I32, (tm, POOL_GROUP), 0)

    def window_sum(ref, cols, gi, j):
        pf = ref[:, cols]
        hi = pf.astype(BF16)
        lo = (pf - hi.astype(F32)).astype(BF16)
        band = band_ref[gi, j]
        return (jnp.dot(band, hi, preferred_element_type=F32)
                + jnp.dot(band, lo, preferred_element_type=F32))

    mixed = []
    for gi, w in enumerate(POOL_WINDOWS):
        cols = slice(gi * POOL_GROUP, (gi + 1) * POOL_GROUP)
        ws = (window_sum(pm_ref, cols, gi, 1) + has_prev * window_sum(pp_ref, cols, gi, 0)
              + has_next * window_sum(pn_ref, cols, gi, 2))
        cnt = (jnp.clip(t + (w - w // 2), 0, seq_len) - jnp.clip(t - w // 2, 0, seq_len)).astype(F32)
        dgi = ws / cnt - pm_ref[:, cols]
        mixed.append(jnp.dot(dgi.astype(BF16), pw_ref[gi], preferred_element_type=F32))
    mixed = jnp.concatenate(mixed, axis=1) * ps_ref[...]
    pool = jnp.dot(mixed.astype(BF16), po_ref[...], preferred_element_type=F32)

    yn = []
    for hd in range(RET_HEADS):
        yh = y_ref[:, hd * RET_DV:(hd + 1) * RET_DV]
        mu = jnp.mean(yh, axis=-1, keepdims=True)
        yc = yh - mu
        var = jnp.mean(yc * yc, axis=-1, keepdims=True)
        yn.append(yc * lax.rsqrt(var + EPS))
    yn = jnp.concatenate(yn, axis=1) * rg_ref[...]
    gate = gz_ref[...].astype(F32)
    ret = jnp.dot((yn * (gate * jax.nn.sigmoid(gate))).astype(BF16), ro_ref[...],
                  preferred_element_type=F32)

    g_pool = mg_ref[:, :D_MODEL].astype(F32)
    g_ret = mg_ref[:, D_MODEL:].astype(F32)
    merged = jax.nn.sigmoid(g_pool) * pool + jax.nn.sigmoid(g_ret) * ret
    out = jnp.dot(merged.astype(BF16), wo_ref[...], preferred_element_type=F32)
    x1 = x_ref[...] + g1_ref[...] * out
    x1_ref[...] = x1
    hf_ref[...] = _rms_mod(x1, ng_ref[...], sh2_ref[...], sc2_ref[...])


def mix_stage(p, y, gz, mg, x, g1, sh2, sc2, pool_w_bf, pool_scale, pool_out_bf, ret_norm_g,
              ret_out_bf, w_out_bf, norm_ffn_g, tm=256):
    b, l, d = x.shape
    nl = l // tm
    bands = _pool_bands(tm)
    rows = lambda w: pl.BlockSpec((None, tm, w), lambda i, j: (i, j, 0))
    vec_b = pl.BlockSpec((None, 1, d), lambda i, j: (i, 0, 0))
    const = lambda a: pl.BlockSpec(a.shape, lambda i, j: (0,) * a.ndim)
    ps = pool_scale.reshape(1, POOL_WIDTH)
    rg = ret_norm_g.reshape(1, V_WIDTH)
    ng = norm_ffn_g.reshape(1, d)
    return pl.pallas_call(
        functools.partial(_mix_body, seq_len=l),
        grid=(b, nl),
        in_specs=[pl.BlockSpec((None, tm, POOL_WIDTH), lambda i, j: (i, jnp.maximum(j - 1, 0), 0)),
                  rows(POOL_WIDTH),
                  pl.BlockSpec((None, tm, POOL_WIDTH), lambda i, j: (i, jnp.minimum(j + 1, nl - 1), 0)),
                  rows(V_WIDTH), rows(V_WIDTH), rows(2 * D_MODEL), rows(d), vec_b, vec_b, vec_b,
                  const(bands), const(pool_w_bf), const(ps), const(pool_out_bf), const(rg),
                  const(ret_out_bf), const(w_out_bf), const(ng)],
        out_specs=[rows(d), rows(d)],
        out_shape=[jax.ShapeDtypeStruct((b, l, d), F32), jax.ShapeDtypeStruct((b, l, d), F32)],
        compiler_params=_params(("parallel", "parallel"), VMEM_LIMIT),
        name="mix",
    )(p, p, p, y, gz, mg, x, g1, sh2, sc2, bands, pool_w_bf, ps, pool_out_bf, rg, ret_out_bf,
      w_out_bf, ng)


def _topk_rows(s, k):
    r, n = s.shape
    rows = lax.broadcasted_iota(I32, (r, n), 0)
    slot = lax.broadcasted_iota(I32, (k, n), 0)
    vals = jnp.zeros((k, n), F32)
    idxs = jnp.zeros((k, n), I32)
    for j in range(k):
        m = jnp.max(s, axis=0, keepdims=True)
        i = jnp.min(jnp.where(s == m, rows, r), axis=0, keepdims=True)
        vals = jnp.where(slot == j, m, vals)
        idxs = jnp.where(slot == j, i, idxs)
        s = jnp.where(rows == i, -jnp.inf, s)
    return vals, idxs


def _pick_rows(table, sel):
    out = jnp.zeros(sel.shape, table.dtype)
    for r in range(table.shape[0]):
        out = jnp.where(sel == r, table[r:r + 1, :], out)
    return out


_HALF_K = PEER_TOPK // 2
assert PEER_TOPK == 16 and SUBLANES == _HALF_K
_CAND_MID = PEER_TOPK + (_HALF_K - 1) * _HALF_K


def _candidates(s1, s2):
    parts = [s1[0:1, :] + s2]
    parts += [s1[i:i + 1, :] + s2[0:_HALF_K, :] for i in range(1, _HALF_K)]
    parts.append(s1[_HALF_K:, :] + s2[0:1, :])
    return jnp.concatenate(parts, axis=0)


def _candidate_ij(pos):
    mid = pos - PEER_TOPK
    i = jnp.where(pos < PEER_TOPK, 0, jnp.where(pos < _CAND_MID, (mid >> 3) + 1, pos - (_CAND_MID - _HALF_K)))
    j = jnp.where(pos < PEER_TOPK, pos, jnp.where(pos < _CAND_MID, mid & (_HALF_K - 1), 0))
    return i, j


def _route_body(hf_ref, wq_ref, keys_ref, e_ref, g_ref):
    tq = hf_ref.shape[0]
    half = PEER_DQ // 2
    q = jnp.dot(hf_ref[...].astype(BF16), wq_ref[...], preferred_element_type=F32).astype(BF16)
    for cb in range(tq // LANES):
        tok = slice(cb * LANES, (cb + 1) * LANES)
        e_heads, g_heads = [], []
        for hd in range(PEER_HEADS):
            sub = []
            for part in range(2):
                col = (hd * 2 + part) * half
                st = lax.dot_general(keys_ref[hd, part], q[tok, col:col + half], _NT,
                                     preferred_element_type=F32)
                sub.append(_topk_rows(st, PEER_TOPK))
            (s1, i1), (s2, i2) = sub
            best, pos = _topk_rows(_candidates(s1, s2), PEER_TOPK)
            ci, cj = _candidate_ij(pos)
            e1 = _pick_rows(i1, ci)
            e2 = _pick_rows(i2, cj)
            ex = jnp.exp(best - best[0:1, :])
            e_heads.append((e1 * PEER_NKEYS + e2) * PACKED_ROWS)
            g_heads.append(ex / jnp.sum(ex, axis=0, keepdims=True))
        e_ref[tok, :] = jnp.concatenate(e_heads, axis=0).T
        g_ref[tok, :] = jnp.concatenate(g_heads, axis=0).T


def route_stage(hf, wq_bf, keys_bf, tq=256):
    n, d = hf.shape
    rows = lambda w: pl.BlockSpec((tq, w), lambda i: (i, 0))
    return pl.pallas_call(
        _route_body,
        grid=(n // tq,),
        in_specs=[rows(d), pl.BlockSpec(wq_bf.shape, lambda i: (0, 0)),
                  pl.BlockSpec(keys_bf.shape, lambda i: (0, 0, 0, 0))],
        out_specs=[rows(PEER_SLOTS), rows(PEER_SLOTS)],
        out_shape=[jax.ShapeDtypeStruct((n, PEER_SLOTS), I32),
                   jax.ShapeDtypeStruct((n, PEER_SLOTS), F32)],
        compiler_params=_params(("parallel",), VMEM_LIMIT),
        name="route",
    )(hf, wq_bf, keys_bf)


def _pack_table(t):
    tb = t.astype(BF16).reshape(PEER_EXPERTS, PACKED_ROWS, 2, LANES).transpose(0, 1, 3, 2)
    return lax.bitcast_convert_type(tb, jnp.uint32).reshape(PEER_EXPERTS * PACKED_ROWS, LANES)


def _table_tile(words):
    return pltpu.bitcast(words, BF16).astype(F32)


OFF_BITS = 16
assert PEER_EXPERTS * PACKED_ROWS <= 1 << OFF_BITS
_WINDOW = 8


def _token_rows(off_ref, tbl_ref, base, paced):
    offs = []
    for j in range(PEER_SLOTS // _WINDOW):
        wbase = base + j * _WINDOW
        if paced and j >= 1:
            wbase = wbase + (offs[-1] >> OFF_BITS)
        offs += [off_ref[wbase + i] for i in range(_WINDOW)]
    return [tbl_ref[pl.ds(pl.multiple_of(o, PACKED_ROWS), PACKED_ROWS), :] for o in offs]


def _peer_u_body(off_ref, h_ref, g_ref, tbl_ref, o_ref, p_scr, s_scr):
    tb = h_ref.shape[0]
    octet = SUBLANES * SUBLANES

    def products(t, buf):
        hv = h_ref[t]
        for k, words in enumerate(_token_rows(off_ref, tbl_ref, t * PEER_SLOTS, paced=False)):
            j, i = divmod(k, SUBLANES)
            p_scr[buf, pl.ds(j * octet + i, SUBLANES, stride=SUBLANES), :] = _table_tile(words) * hv

    def reduce(buf, t):
        for j in range(PEER_SLOTS // SUBLANES):
            acc = p_scr[buf, pl.ds(j * octet, SUBLANES), :]
            for s in range(1, SUBLANES):
                acc = acc + p_scr[buf, pl.ds(j * octet + s * SUBLANES, SUBLANES), :]
            s_scr[pl.ds(pl.multiple_of(t * PEER_SLOTS + j * SUBLANES, SUBLANES), SUBLANES), :] = acc

    p_scr[1] = jnp.zeros(p_scr.shape[1:], F32)

    def pair(i, carry):
        t = 2 * i
        reduce(1, jnp.maximum(t - 1, 0))
        products(t, 0)
        reduce(0, t)
        products(t + 1, 1)
        return carry

    lax.fori_loop(0, tb // 2, pair, 0)
    reduce(1, tb - 1)

    eye = (lax.broadcasted_iota(I32, (PEER_SLOTS, LANES), 0)
           == lax.broadcasted_iota(I32, (PEER_SLOTS, LANES), 1))

    def finish(c, carry):
        r0 = pl.multiple_of(c * SUBLANES, SUBLANES)
        acts = []
        for i in range(SUBLANES):
            part = s_scr[pl.ds(pl.multiple_of((r0 + i) * PEER_SLOTS, PEER_SLOTS), PEER_SLOTS), :]
            tot = jnp.sum(part, axis=1, keepdims=True)
            acts.append(jnp.sum(jnp.where(eye, tot, 0.0), axis=0, keepdims=True))
        act = jnp.concatenate(acts, axis=0)
        rows8 = pl.ds(r0, SUBLANES)
        o_ref[rows8, :] = g_ref[rows8, :] * (0.5 * act * (1.0 + lax.erf(act * (2.0 ** -0.5))))
        return carry

    lax.fori_loop(0, tb // SUBLANES, finish, 0)


def _smem_slots(tb):
    return pl.BlockSpec((tb * PEER_SLOTS,), lambda i: (i,), memory_space=pltpu.SMEM)


def peer_u_stage(eoff, hf3, gates, tbl, tb=128):
    n = eoff.shape[0]
    rows = lambda: pl.BlockSpec((tb, PEER_SLOTS), lambda i: (i, 0))
    return pl.pallas_call(
        _peer_u_body,
        grid=(n // tb,),
        in_specs=[_smem_slots(tb), pl.BlockSpec((tb, ROW_TILES, LANES), lambda i: (i, 0, 0)), rows(),
                  _resident(tbl.shape)],
        out_specs=rows(),
        out_shape=jax.ShapeDtypeStruct((n, PEER_SLOTS), F32),
        scratch_shapes=[pltpu.VMEM((2, PEER_SLOTS * SUBLANES, LANES), F32),
                        pltpu.VMEM((tb * PEER_SLOTS, LANES), F32)],
        compiler_params=_params(("arbitrary",), VMEM_LIMIT),
        name="peer_u",
    )(eoff.reshape(-1), hf3, gates, tbl)


_V_ACCS = 4


def _peer_v_body(off_ref, w_ref, x_ref, g2_ref, fg_ref, tbl_ref, o_ref):
    tb = x_ref.shape[0]

    def token(t, carry):
        wrep = jnp.broadcast_to(w_ref[t], (LANES, PEER_SLOTS)).T
        accs = [None] * _V_ACCS
        for k, words in enumerate(_token_rows(off_ref, tbl_ref, t * PEER_SLOTS, paced=True)):
            term = jnp.broadcast_to(wrep[k:k + 1, :], (ROW_TILES, LANES)) * _table_tile(words)
            accs[k % _V_ACCS] = term if accs[k % _V_ACCS] is None else accs[k % _V_ACCS] + term
        o_ref[t] = (accs[0] + accs[1]) + (accs[2] + accs[3])
        return carry

    lax.fori_loop(0, tb, token, 0)
    x2 = x_ref[...] + g2_ref[...] * o_ref[...]
    ms = jnp.sum(jnp.sum(x2 * x2, axis=2, keepdims=True), axis=1, keepdims=True) / D_MODEL
    o_ref[...] = (x2 * lax.rsqrt(ms + EPS)) * fg_ref[...]


def peer_v_stage(eoff, w, x3, g2, final_g, tbl, tokens_per_batch, tb=128):
    n = eoff.shape[0]
    per_b = tokens_per_batch // tb
    tile = lambda: pl.BlockSpec((tb, ROW_TILES, LANES), lambda i: (i, 0, 0))
    return pl.pallas_call(
        _peer_v_body,
        grid=(n // tb,),
        in_specs=[_smem_slots(tb), pl.BlockSpec((tb, 1, PEER_SLOTS), lambda i: (i, 0, 0)), tile(),
                  pl.BlockSpec((None, ROW_TILES, LANES), lambda i: (i // per_b, 0, 0)),
                  pl.BlockSpec((ROW_TILES, LANES), lambda i: (0, 0)),
                  _resident(tbl.shape)],
        out_specs=tile(),
        out_shape=jax.ShapeDtypeStruct((n, ROW_TILES, LANES), F32),
        compiler_params=_params(("arbitrary",), VMEM_LIMIT),
        name="peer_v",
    )(eoff.reshape(-1), w.reshape(n, 1, PEER_SLOTS), x3, g2, final_g.reshape(ROW_TILES, LANES), tbl)


def _rope_tables(l):
    quarter = RET_DK // 4
    rows = l // GRID_W
    row = jnp.repeat(jnp.arange(rows, dtype=F32), GRID_W)
    col = jnp.tile(jnp.arange(GRID_W, dtype=F32), rows)
    inv = ROPE_BASE ** (-jnp.arange(quarter, dtype=F32) / quarter)
    ang = jnp.concatenate([row[:, None] * inv, col[:, None] * inv], axis=-1)
    cos, sin = jnp.cos(ang), jnp.sin(ang)
    return jnp.concatenate([cos, cos], axis=-1), jnp.concatenate([-sin, sin], axis=-1)


def _layer(x, ctx, c, c_ctx, ada_w, ada_b, norm_mix_g, norm_ffn_g, w_in, pool_w, pool_scale,
           pool_out, ret_decay, ret_norm_g, ret_out, w_out, peer_wq, peer_keys, peer_u, peer_v,
           final_g):
    b, l, d = x.shape
    n = b * l

    rows = -(-(b + 1) // SUBLANES) * SUBLANES
    cc = jnp.zeros((rows, d), F32).at[:b].set(c).at[b].set(c_ctx)
    mod = ada_stage(cc, ada_w, ada_b)
    sh1, sc1, g1, sh2, sc2, g2 = [m.reshape(b, 1, d) for m in jnp.split(mod[:b], 6, axis=-1)]
    csh1, csc1 = mod[b, :d], mod[b, d:2 * d]
    lg = jax.nn.log_sigmoid(ret_decay.astype(F32))

    w_in_bf = w_in.astype(BF16)
    s_f, s_b = ctx_stage(lg, ctx, norm_mix_g, csh1, csc1, w_in_bf[:, OFF_K:OFF_G])

    cos, sin = _rope_tables(l)
    p, q, k, v, gz, mg = proj_stage(x, norm_mix_g, sh1, sc1, cos, sin, w_in_bf)
    y = ret_stage(lg, q, k, v, s_f, s_b)
    x1, hf = mix_stage(p, y, gz, mg, x, g1, sh2, sc2, pool_w.astype(BF16), pool_scale,
                       pool_out.astype(BF16), ret_norm_g, ret_out.astype(BF16),
                       w_out.astype(BF16), norm_ffn_g)

    hf2 = hf.reshape(n, d)
    eoff, gates = route_stage(hf2, peer_wq.astype(BF16), peer_keys.astype(BF16))
    w = peer_u_stage(eoff, hf2.reshape(n, ROW_TILES, LANES), gates, _pack_table(peer_u))
    out = peer_v_stage(eoff, w, x1.reshape(n, ROW_TILES, LANES), g2.reshape(b, ROW_TILES, LANES),
                       final_g, _pack_table(peer_v), l)
    return out.reshape(b, l, d)


def kernel(x, c, ctx, c_ctx, ada_w, ada_b, norm_mix_g, norm_ffn_g, w_in, pool_w, pool_scale, pool_out, ret_decay, ret_norm_g, ret_out, w_out, peer_wq, peer_keys, peer_u, peer_v, final_g):
    assert ada_w.shape[0] == 1, "single-layer block"
    return _layer(x, ctx, c, c_ctx, ada_w[0], ada_b[0], norm_mix_g[0], norm_ffn_g[0], w_in[0],
                  pool_w[0], pool_scale[0], pool_out[0], ret_decay[0], ret_norm_g[0], ret_out[0],
                  w_out[0], peer_wq[0], peer_keys[0], peer_u[0], peer_v[0], final_g)
```

```python
import functools

import jax
import jax.numpy as jnp
import numpy as np
from jax import lax
from jax.experimental import pallas as pl
from jax.experimental.pallas import tpu as pltpu

F32 = jnp.float32
BF16 = jnp.bfloat16
I32 = jnp.int32

D_MODEL = 1024
GRID_W = 64
EPS = 1e-6

POOL_WINDOWS = (2, 4, 8, 16)
POOL_WIDTH = D_MODEL // 2
POOL_GROUP = POOL_WIDTH // len(POOL_WINDOWS)

RET_HEADS = 4
RET_DK = 128
RET_DV = 256
RET_CHUNK = 128
ROPE_BASE = 10000.0
QK_WIDTH = RET_HEADS * RET_DK
V_WIDTH = RET_HEADS * RET_DV
K_SCALE = RET_DK ** -0.5

OFF_POOL = 0
OFF_Q = OFF_POOL + POOL_WIDTH
OFF_K = OFF_Q + QK_WIDTH
OFF_V = OFF_K + QK_WIDTH
OFF_G = OFF_V + V_WIDTH
OFF_MERGE = OFF_G + V_WIDTH
IN_WIDTH = OFF_MERGE + 2 * D_MODEL

PEER_HEADS = 8
PEER_NKEYS = 128
PEER_EXPERTS = PEER_NKEYS * PEER_NKEYS
PEER_TOPK = 16
PEER_DQ = 256
PEER_SLOTS = PEER_HEADS * PEER_TOPK

LANES = 128
SUBLANES = 8
ROW_TILES = D_MODEL // LANES
PACKED_ROWS = ROW_TILES // 2
VMEM_LIMIT = 56 * 1024 * 1024

_NT = (((1,), (1,)), ((), ()))
_TN = (((0,), (0,)), ((), ()))


def _params(sem, vmem=None):
    return pltpu.CompilerParams(dimension_semantics=sem, vmem_limit_bytes=vmem)


def _resident(shape):
    nd = len(shape)
    return pl.BlockSpec(shape, lambda *_: (0,) * nd, pipeline_mode=pl.Buffered(1))


def _rms_mod(xf, g, shift, scale):
    y = xf * lax.rsqrt(jnp.mean(xf * xf, axis=-1, keepdims=True) + EPS)
    return (y * g) * (1.0 + scale) + shift


def _ada_body(c_ref, w_ref, b_ref, o_ref):
    c = c_ref[...]
    s = c * jax.nn.sigmoid(c)
    o_ref[...] = jnp.dot(s, w_ref[...], preferred_element_type=F32,
                         precision=lax.Precision.HIGHEST) + b_ref[...]


def ada_stage(cc, ada_w, ada_b, tn=512):
    r, d = cc.shape
    n = ada_w.shape[1]
    return pl.pallas_call(
        _ada_body,
        grid=(n // tn,),
        in_specs=[pl.BlockSpec((r, d), lambda j: (0, 0)),
                  pl.BlockSpec((d, tn), lambda j: (0, j)),
                  pl.BlockSpec((1, tn), lambda j: (0, j))],
        out_specs=pl.BlockSpec((r, tn), lambda j: (0, j)),
        out_shape=jax.ShapeDtypeStruct((r, n), F32),
        compiler_params=_params(("parallel",)),
        name="ada",
    )(cc, ada_w, ada_b.reshape(1, n))


def _ctx_body(lg_ref, ctx_ref, g_ref, sh_ref, sc_ref, w_ref, sf_ref, sb_ref):
    lc = ctx_ref.shape[0]
    hc = _rms_mod(ctx_ref[...], g_ref[...], sh_ref[...], sc_ref[...])
    kv = jnp.dot(hc.astype(BF16), w_ref[...], preferred_element_type=F32)
    m = lax.broadcasted_iota(I32, (lc, RET_DK), 0).astype(F32)
    for h in range(RET_HEADS):
        kf = kv[:, h * RET_DK:(h + 1) * RET_DK] * K_SCALE
        vb = kv[:, QK_WIDTH + h * RET_DV:QK_WIDTH + (h + 1) * RET_DV].astype(BF16)
        wf = jnp.exp(lg_ref[0, h] * (lc - 1.0 - m))
        wb = jnp.exp(lg_ref[1, h] * m)
        sf_ref[h] = lax.dot_general((kf * wf).astype(BF16), vb, _TN, preferred_element_type=F32)
        sb_ref[h] = lax.dot_general((kf * wb).astype(BF16), vb, _TN, preferred_element_type=F32)


def ctx_stage(lg, ctx, norm_g, csh, csc, w_kv):
    b, lc, d = ctx.shape
    vec = pl.BlockSpec((1, d), lambda i: (0, 0))
    st = jax.ShapeDtypeStruct((b, RET_HEADS, RET_DK, RET_DV), F32)
    st_spec = pl.BlockSpec((None, RET_HEADS, RET_DK, RET_DV), lambda i: (i, 0, 0, 0))
    return pl.pallas_call(
        _ctx_body,
        grid=(b,),
        in_specs=[pl.BlockSpec(memory_space=pltpu.SMEM),
                  pl.BlockSpec((None, lc, d), lambda i: (i, 0, 0)),
                  vec, vec, vec,
                  pl.BlockSpec(w_kv.shape, lambda i: (0, 0))],
        out_specs=[st_spec, st_spec],
        out_shape=[st, st],
        compiler_params=_params(("parallel",)),
        name="ctx",
    )(lg, ctx, norm_g.reshape(1, d), csh.reshape(1, d), csc.reshape(1, d), w_kv)


def _rope(a, cos, sin_signed):
    return a * cos + pltpu.roll(a, RET_DK // 2, 1) * sin_signed


def _proj_body(x_ref, g_ref, sh_ref, sc_ref, cos_ref, sin_ref, w_ref,
               p_ref, q_ref, k_ref, v_ref, gz_ref, mg_ref):
    h = _rms_mod(x_ref[...], g_ref[...], sh_ref[...], sc_ref[...]).astype(BF16)

    def mm(lo, hi):
        return jnp.dot(h, w_ref[:, lo:hi], preferred_element_type=F32)

    p_ref[...] = mm(OFF_POOL, OFF_Q)
    cos = cos_ref[...]
    sin = sin_ref[...]
    qf = mm(OFF_Q, OFF_K)
    kf = mm(OFF_K, OFF_V)
    for hd in range(RET_HEADS):
        sl = slice(hd * RET_DK, (hd + 1) * RET_DK)
        q_ref[:, sl] = _rope(qf[:, sl], cos, sin).astype(BF16)
        k_ref[:, sl] = (_rope(kf[:, sl], cos, sin) * K_SCALE).astype(BF16)
    v_ref[...] = mm(OFF_V, OFF_G).astype(BF16)
    gz_ref[...] = mm(OFF_G, OFF_MERGE).astype(BF16)
    mg_ref[...] = mm(OFF_MERGE, IN_WIDTH).astype(BF16)


def proj_stage(x, norm_g, sh1, sc1, cos, sin, w_in_bf, tm=512):
    b, l, d = x.shape
    vec_b = pl.BlockSpec((None, 1, d), lambda i, j: (i, 0, 0))
    rows = lambda w: pl.BlockSpec((None, tm, w), lambda i, j: (i, j, 0))
    tab = pl.BlockSpec((tm, RET_DK), lambda i, j: (j, 0))
    outs = [(POOL_WIDTH, F32), (QK_WIDTH, BF16), (QK_WIDTH, BF16), (V_WIDTH, BF16),
            (V_WIDTH, BF16), (2 * D_MODEL, BF16)]
    return pl.pallas_call(
        _proj_body,
        grid=(b, l // tm),
        in_specs=[rows(d), pl.BlockSpec((1, d), lambda i, j: (0, 0)), vec_b, vec_b, tab, tab,
                  _resident(w_in_bf.shape)],
        out_specs=[rows(w) for w, _ in outs],
        out_shape=[jax.ShapeDtypeStruct((b, l, w), dt) for w, dt in outs],
        compiler_params=_params(("parallel", "parallel"), VMEM_LIMIT),
        name="proj",
    )(x, norm_g.reshape(1, d), sh1, sc1, cos, sin, w_in_bf)


def _ret_body(lg_ref, q_ref, k_ref, v_ref, sf_ref, sb_ref, y_ref, s_scr):
    c = RET_CHUNK
    n_chunks = q_ref.shape[0] // c
    hd = pl.program_id(1)
    lgf = lg_ref[0, hd]
    lgb = lg_ref[1, hd]
    n_i = lax.broadcasted_iota(I32, (c, c), 0)
    m_i = lax.broadcasted_iota(I32, (c, c), 1)
    rel = (n_i - m_i).astype(F32)
    intra_f = jnp.where(rel >= 0, jnp.exp(lgf * jnp.where(rel >= 0, rel, 0.0)), 0.0)
    intra_b = jnp.where(rel < 0, jnp.exp(lgb * jnp.where(rel < 0, -rel, 0.0)), 0.0)
    pos = lax.broadcasted_iota(I32, (c, RET_DK), 0).astype(F32)
    qdec_f = jnp.exp(lgf * (pos + 1.0))
    kdec_f = jnp.exp(lgf * (c - 1.0 - pos))
    qdec_b = jnp.exp(lgb * (c - pos))
    kdec_b = jnp.exp(lgb * pos)
    blk_f = jnp.exp(jnp.full((1, RET_DV), lgf * c, F32))
    blk_b = jnp.exp(jnp.full((1, RET_DV), lgb * c, F32))

    def chunk(i, intra, qdec, kdec, blk):
        rows = pl.ds(pl.multiple_of(i * c, c), c)
        qi = q_ref[rows, :]
        ki = k_ref[rows, :]
        vi = v_ref[rows, :]
        sc = lax.dot_general(qi, ki, _NT, preferred_element_type=F32) * intra
        s = s_scr[...]
        y = (jnp.dot(sc.astype(BF16), vi, preferred_element_type=F32)
             + jnp.dot((qi.astype(F32) * qdec).astype(BF16), s.astype(BF16),
                       preferred_element_type=F32))
        s_scr[...] = s * blk + lax.dot_general((ki.astype(F32) * kdec).astype(BF16), vi, _TN,
                                               preferred_element_type=F32)
        return rows, y

    s_scr[...] = sf_ref[...]

    def fwd(i, carry):
        rows, y = chunk(i, intra_f, qdec_f, kdec_f, blk_f)
        y_ref[rows, :] = y
        return carry

    lax.fori_loop(0, n_chunks, fwd, 0)
    s_scr[...] = sb_ref[...]

    def bwd(j, carry):
        rows, y = chunk(n_chunks - 1 - j, intra_b, qdec_b, kdec_b, blk_b)
        y_ref[rows, :] += y
        return carry

    lax.fori_loop(0, n_chunks, bwd, 0)


def ret_stage(lg, q, k, v, s_f, s_b):
    b, l, _ = q.shape
    qk_spec = pl.BlockSpec((None, l, RET_DK), lambda i, j: (i, 0, j))
    v_spec = pl.BlockSpec((None, l, RET_DV), lambda i, j: (i, 0, j))
    st_spec = pl.BlockSpec((None, None, RET_DK, RET_DV), lambda i, j: (i, j, 0, 0))
    return pl.pallas_call(
        _ret_body,
        grid=(b, RET_HEADS),
        in_specs=[pl.BlockSpec(memory_space=pltpu.SMEM), qk_spec, qk_spec, v_spec, st_spec, st_spec],
        out_specs=v_spec,
        out_shape=jax.ShapeDtypeStruct((b, l, V_WIDTH), F32),
        scratch_shapes=[pltpu.VMEM((RET_DK, RET_DV), F32)],
        compiler_params=_params(("parallel", "parallel")),
        name="ret",
    )(lg, q, k, v, s_f, s_b)


def _pool_bands(tm):
    r = np.arange(tm)[:, None]
    c = np.arange(tm)[None, :]
    bands = np.zeros((len(POOL_WINDOWS), 3, tm, tm), np.float32)
    for gi, w in enumerate(POOL_WINDOWS):
        lo, hi = r - w // 2, r + w - w // 2
        for j, off in enumerate((-tm, 0, tm)):
            bands[gi, j] = ((c + off >= lo) & (c + off < hi)).astype(np.float32)
    return jnp.asarray(bands, BF16)


def _mix_body(pp_ref, pm_ref, pn_ref, y_ref, gz_ref, mg_ref, x_ref, g1_ref, sh2_ref, sc2_ref,
              band_ref, pw_ref, ps_ref, po_ref, rg_ref, ro_ref, wo_ref, ng_ref,
              x1_ref, hf_ref, *, seq_len):
    tm = pm_ref.shape[0]
    li = pl.program_id(1)
    has_prev = (li > 0).astype(F32)
    has_next = (li < pl.num_programs(1) - 1).astype(F32)
    t = li * tm + lax.broadcasted_iota(I32, (tm, POOL_GROUP), 0)

    def window_sum(ref, cols, gi, j):
        pf = ref[:, cols]
        hi = pf.astype(BF16)
        lo = (pf - hi.astype(F32)).astype(BF16)
        band = band_ref[gi, j]
        return (jnp.dot(band, hi, preferred_element_type=F32)
                + jnp.dot(band, lo, preferred_element_type=F32))

    mixed = []
    for gi, w in enumerate(POOL_WINDOWS):
        cols = slice(gi * POOL_GROUP, (gi + 1) * POOL_GROUP)
        ws = (window_sum(pm_ref, cols, gi, 1) + has_prev * window_sum(pp_ref, cols, gi, 0)
              + has_next * window_sum(pn_ref, cols, gi, 2))
        cnt = (jnp.clip(t + (w - w // 2), 0, seq_len) - jnp.clip(t - w // 2, 0, seq_len)).astype(F32)
        dgi = ws / cnt - pm_ref[:, cols]
        mixed.append(jnp.dot(dgi.astype(BF16), pw_ref[gi], preferred_element_type=F32))
    mixed = jnp.concatenate(mixed, axis=1) * ps_ref[...]
    pool = jnp.dot(mixed.astype(BF16), po_ref[...], preferred_element_type=F32)

    yn = []
    for hd in range(RET_HEADS):
        yh = y_ref[:, hd * RET_DV:(hd + 1) * RET_DV]
        mu = jnp.mean(yh, axis=-1, keepdims=True)
        yc = yh - mu
        var = jnp.mean(yc * yc, axis=-1, keepdims=True)
        yn.append(yc * lax.rsqrt(var + EPS))
    yn = jnp.concatenate(yn, axis=1) * rg_ref[...]
    gate = gz_ref[...].astype(F32)
    ret = jnp.dot((yn * (gate * jax.nn.sigmoid(gate))).astype(BF16), ro_ref[...],
                  preferred_element_type=F32)

    g_pool = mg_ref[:, :D_MODEL].astype(F32)
    g_ret = mg_ref[:, D_MODEL:].astype(F32)
    merged = jax.nn.sigmoid(g_pool) * pool + jax.nn.sigmoid(g_ret) * ret
    out = jnp.dot(merged.astype(BF16), wo_ref[...], preferred_element_type=F32)
    x1 = x_ref[...] + g1_ref[...] * out
    x1_ref[...] = x1
    hf_ref[...] = _rms_mod(x1, ng_ref[...], sh2_ref[...], sc2_ref[...])


def mix_stage(p, y, gz, mg, x, g1, sh2, sc2, pool_w_bf, pool_scale, pool_out_bf, ret_norm_g,
              ret_out_bf, w_out_bf, norm_ffn_g, tm=256):
    b, l, d = x.shape
    nl = l // tm
    bands = _pool_bands(tm)
    rows = lambda w: pl.BlockSpec((None, tm, w), lambda i, j: (i, j, 0))
    vec_b = pl.BlockSpec((None, 1, d), lambda i, j: (i, 0, 0))
    const = lambda a: pl.BlockSpec(a.shape, lambda i, j: (0,) * a.ndim)
    ps = pool_scale.reshape(1, POOL_WIDTH)
    rg = ret_norm_g.reshape(1, V_WIDTH)
    ng = norm_ffn_g.reshape(1, d)
    return pl.pallas_call(
        functools.partial(_mix_body, seq_len=l),
        grid=(b, nl),
        in_specs=[pl.BlockSpec((None, tm, POOL_WIDTH), lambda i, j: (i, jnp.maximum(j - 1, 0), 0)),
                  rows(POOL_WIDTH),
                  pl.BlockSpec((None, tm, POOL_WIDTH), lambda i, j: (i, jnp.minimum(j + 1, nl - 1), 0)),
                  rows(V_WIDTH), rows(V_WIDTH), rows(2 * D_MODEL), rows(d), vec_b, vec_b, vec_b,
                  const(bands), const(pool_w_bf), const(ps), const(pool_out_bf), const(rg),
                  const(ret_out_bf), const(w_out_bf), const(ng)],
        out_specs=[rows(d), rows(d)],
        out_shape=[jax.ShapeDtypeStruct((b, l, d), F32), jax.ShapeDtypeStruct((b, l, d), F32)],
        compiler_params=_params(("parallel", "parallel"), VMEM_LIMIT),
        name="mix",
    )(p, p, p, y, gz, mg, x, g1, sh2, sc2, bands, pool_w_bf, ps, pool_out_bf, rg, ret_out_bf,
      w_out_bf, ng)


def _topk_rows(s, k):
    r, n = s.shape
    rows = lax.broadcasted_iota(I32, (r, n), 0)
    slot = lax.broadcasted_iota(I32, (k, n), 0)
    vals = jnp.zeros((k, n), F32)
    idxs = jnp.zeros((k, n), I32)
    for j in range(k):
        m = jnp.max(s, axis=0, keepdims=True)
        i = jnp.min(jnp.where(s == m, rows, r), axis=0, keepdims=True)
        vals = jnp.where(slot == j, m, vals)
        idxs = jnp.where(slot == j, i, idxs)
        s = jnp.where(rows == i, -jnp.inf, s)
    return vals, idxs


def _pick_rows(table, sel):
    out = jnp.zeros(sel.shape, table.dtype)
    for r in range(table.shape[0]):
        out = jnp.where(sel == r, table[r:r + 1, :], out)
    return out


_HALF_K = PEER_TOPK // 2
assert PEER_TOPK == 16 and SUBLANES == _HALF_K
_CAND_MID = PEER_TOPK + (_HALF_K - 1) * _HALF_K


def _candidates(s1, s2):
    parts = [s1[0:1, :] + s2]
    parts += [s1[i:i + 1, :] + s2[0:_HALF_K, :] for i in range(1, _HALF_K)]
    parts.append(s1[_HALF_K:, :] + s2[0:1, :])
    return jnp.concatenate(parts, axis=0)


def _candidate_ij(pos):
    mid = pos - PEER_TOPK
    i = jnp.where(pos < PEER_TOPK, 0, jnp.where(pos < _CAND_MID, (mid >> 3) + 1, pos - (_CAND_MID - _HALF_K)))
    j = jnp.where(pos < PEER_TOPK, pos, jnp.where(pos < _CAND_MID, mid & (_HALF_K - 1), 0))
    return i, j


def _route_body(hf_ref, wq_ref, keys_ref, e_ref, g_ref):
    tq = hf_ref.shape[0]
    half = PEER_DQ // 2
    q = jnp.dot(hf_ref[...].astype(BF16), wq_ref[...], preferred_element_type=F32).astype(BF16)
    for cb in range(tq // LANES):
        tok = slice(cb * LANES, (cb + 1) * LANES)
        e_heads, g_heads = [], []
        for hd in range(PEER_HEADS):
            sub = []
            for part in range(2):
                col = (hd * 2 + part) * half
                st = lax.dot_general(keys_ref[hd, part], q[tok, col:col + half], _NT,
                                     preferred_element_type=F32)
                sub.append(_topk_rows(st, PEER_TOPK))
            (s1, i1), (s2, i2) = sub
            best, pos = _topk_rows(_candidates(s1, s2), PEER_TOPK)
            ci, cj = _candidate_ij(pos)
            e1 = _pick_rows(i1, ci)
            e2 = _pick_rows(i2, cj)
            ex = jnp.exp(best - best[0:1, :])
            e_heads.append((e1 * PEER_NKEYS + e2) * PACKED_ROWS)
            g_heads.append(ex / jnp.sum(ex, axis=0, keepdims=True))
        e_ref[tok, :] = jnp.concatenate(e_heads, axis=0).T
        g_ref[tok, :] = jnp.concatenate(g_heads, axis=0).T


def route_stage(hf, wq_bf, keys_bf, tq=256):
    n, d = hf.shape
    rows = lambda w: pl.BlockSpec((tq, w), lambda i: (i, 0))
    return pl.pallas_call(
        _route_body,
        grid=(n // tq,),
        in_specs=[rows(d), pl.BlockSpec(wq_bf.shape, lambda i: (0, 0)),
                  pl.BlockSpec(keys_bf.shape, lambda i: (0, 0, 0, 0))],
        out_specs=[rows(PEER_SLOTS), rows(PEER_SLOTS)],
        out_shape=[jax.ShapeDtypeStruct((n, PEER_SLOTS), I32),
                   jax.ShapeDtypeStruct((n, PEER_SLOTS), F32)],
        compiler_params=_params(("parallel",), VMEM_LIMIT),
        name="route",
    )(hf, wq_bf, keys_bf)


def _pack_table(t):
    tb = t.astype(BF16).reshape(PEER_EXPERTS, PACKED_ROWS, 2, LANES).transpose(0, 1, 3, 2)
    return lax.bitcast_convert_type(tb, jnp.uint32).reshape(PEER_EXPERTS * PACKED_ROWS, LANES)


def _table_tile(words):
    return pltpu.bitcast(words, BF16).astype(F32)


OFF_BITS = 16
assert PEER_EXPERTS * PACKED_ROWS <= 1 << OFF_BITS
PEER_TB = 32
_HALF_TB = PEER_TB // 2
_HALF_WORDS = _HALF_TB * PEER_SLOTS


def _offset_copy(off_hbm, bufs, sems, step, h):
    start = (2 * step + h) * _HALF_WORDS
    return pltpu.make_async_copy(off_hbm.at[pl.ds(start, _HALF_WORDS)], bufs[h], sems.at[h])


def _for_each_half(off_hbm, bufs, sems, half_fn):
    step = pl.program_id(0)

    @pl.when(step == 0)
    def _():
        _offset_copy(off_hbm, bufs, sems, step, 0).start()

    _offset_copy(off_hbm, bufs, sems, step, 1).start()
    _offset_copy(off_hbm, bufs, sems, step, 0).wait()
    half_fn(bufs[0], 0)

    @pl.when(step + 1 < pl.num_programs(0))
    def _():
        _offset_copy(off_hbm, bufs, sems, step + 1, 0).start()

    _offset_copy(off_hbm, bufs, sems, step, 1).wait()
    half_fn(bufs[1], _HALF_TB)


def _gather_row(tbl_ref, off):
    return tbl_ref[pl.ds(pl.multiple_of(off, PACKED_ROWS), PACKED_ROWS), :]


_OFFSET_SCRATCH = [pltpu.SMEM((_HALF_WORDS,), I32), pltpu.SMEM((_HALF_WORDS,), I32),
                   pltpu.SemaphoreType.DMA((2,))]


def _peer_u_body(off_hbm, h_ref, g_ref, tbl_ref, o_ref, idx_a, idx_b, sems, p_scr, s_scr):
    octet = SUBLANES * SUBLANES
    eye = (lax.broadcasted_iota(I32, (PEER_SLOTS, LANES), 0)
           == lax.broadcasted_iota(I32, (PEER_SLOTS, LANES), 1))

    def half(idx_ref, first):
        for t in range(_HALF_TB):
            tok = first + t
            buf = tok % 2
            hv = h_ref[tok]
            for k in range(PEER_SLOTS):
                j, i = divmod(k, SUBLANES)
                prod = _table_tile(_gather_row(tbl_ref, idx_ref[t * PEER_SLOTS + k])) * hv
                p_scr[buf, pl.ds(j * octet + i, SUBLANES, stride=SUBLANES), :] = prod
            for j in range(PEER_SLOTS // SUBLANES):
                acc = p_scr[buf, pl.ds(j * octet, SUBLANES), :]
                for s in range(1, SUBLANES):
                    acc = acc + p_scr[buf, pl.ds(j * octet + s * SUBLANES, SUBLANES), :]
                s_scr[pl.ds(tok * PEER_SLOTS + j * SUBLANES, SUBLANES), :] = acc
        for r0 in range(first, first + _HALF_TB, SUBLANES):
            acts = []
            for i in range(SUBLANES):
                tot = jnp.sum(s_scr[pl.ds((r0 + i) * PEER_SLOTS, PEER_SLOTS), :], axis=1, keepdims=True)
                acts.append(jnp.sum(jnp.where(eye, tot, 0.0), axis=0, keepdims=True))
            act = jnp.concatenate(acts, axis=0)
            rows8 = pl.ds(r0, SUBLANES)
            o_ref[rows8, :] = g_ref[rows8, :] * (0.5 * act * (1.0 + lax.erf(act * (2.0 ** -0.5))))

    _for_each_half(off_hbm, (idx_a, idx_b), sems, half)


def peer_u_stage(eoff, hf3, gates, tbl):
    n = eoff.shape[0]
    rows = lambda: pl.BlockSpec((PEER_TB, PEER_SLOTS), lambda i: (i, 0))
    return pl.pallas_call(
        _peer_u_body,
        grid=(n // PEER_TB,),
        in_specs=[pl.BlockSpec(memory_space=pl.ANY),
                  pl.BlockSpec((PEER_TB, ROW_TILES, LANES), lambda i: (i, 0, 0)), rows(),
                  _resident(tbl.shape)],
        out_specs=rows(),
        out_shape=jax.ShapeDtypeStruct((n, PEER_SLOTS), F32),
        scratch_shapes=_OFFSET_SCRATCH + [pltpu.VMEM((2, PEER_SLOTS * SUBLANES, LANES), F32),
                                          pltpu.VMEM((PEER_TB * PEER_SLOTS, LANES), F32)],
        compiler_params=_params(("arbitrary",), VMEM_LIMIT),
        name="peer_u",
    )(eoff.reshape(-1), hf3, gates, tbl)


_V_ACCS = 4


def _peer_v_body(off_hbm, w_ref, x_ref, g2_ref, fg_ref, tbl_ref, o_ref, idx_a, idx_b, sems):
    def half(idx_ref, first):
        for t in range(_HALF_TB):
            tok = first + t
            wrep = jnp.broadcast_to(w_ref[tok], (LANES, PEER_SLOTS)).T
            accs = [None] * _V_ACCS
            for k in range(PEER_SLOTS):
                row = _table_tile(_gather_row(tbl_ref, idx_ref[t * PEER_SLOTS + k]))
                term = jnp.broadcast_to(wrep[k:k + 1, :], (ROW_TILES, LANES)) * row
                accs[k % _V_ACCS] = term if accs[k % _V_ACCS] is None else accs[k % _V_ACCS] + term
            o_ref[tok] = (accs[0] + accs[1]) + (accs[2] + accs[3])
        rows = pl.ds(first, _HALF_TB)
        x2 = x_ref[rows] + g2_ref[...] * o_ref[rows]
        ms = jnp.sum(jnp.sum(x2 * x2, axis=2, keepdims=True), axis=1, keepdims=True) / D_MODEL
        o_ref[rows] = (x2 * lax.rsqrt(ms + EPS)) * fg_ref[...]

    _for_each_half(off_hbm, (idx_a, idx_b), sems, half)


def peer_v_stage(eoff, w, x3, g2, final_g, tbl, tokens_per_batch):
    n = eoff.shape[0]
    per_b = tokens_per_batch // PEER_TB
    tile = lambda: pl.BlockSpec((PEER_TB, ROW_TILES, LANES), lambda i: (i, 0, 0))
    return pl.pallas_call(
        _peer_v_body,
        grid=(n // PEER_TB,),
        in_specs=[pl.BlockSpec(memory_space=pl.ANY),
                  pl.BlockSpec((PEER_TB, 1, PEER_SLOTS), lambda i: (i, 0, 0)), tile(),
                  pl.BlockSpec((None, ROW_TILES, LANES), lambda i: (i // per_b, 0, 0)),
                  pl.BlockSpec((ROW_TILES, LANES), lambda i: (0, 0)),
                  _resident(tbl.shape)],
        out_specs=tile(),
        out_shape=jax.ShapeDtypeStruct((n, ROW_TILES, LANES), F32),
        scratch_shapes=list(_OFFSET_SCRATCH),
        compiler_params=_params(("arbitrary",), VMEM_LIMIT),
        name="peer_v",
    )(eoff.reshape(-1), w.reshape(n, 1, PEER_SLOTS), x3, g2, final_g.reshape(ROW_TILES, LANES), tbl)


def _rope_tables(l):
    quarter = RET_DK // 4
    rows = l // GRID_W
    row = jnp.repeat(jnp.arange(rows, dtype=F32), GRID_W)
    col = jnp.tile(jnp.arange(GRID_W, dtype=F32), rows)
    inv = ROPE_BASE ** (-jnp.arange(quarter, dtype=F32) / quarter)
    ang = jnp.concatenate([row[:, None] * inv, col[:, None] * inv], axis=-1)
    cos, sin = jnp.cos(ang), jnp.sin(ang)
    return jnp.concatenate([cos, cos], axis=-1), jnp.concatenate([-sin, sin], axis=-1)


def _layer(x, ctx, c, c_ctx, ada_w, ada_b, norm_mix_g, norm_ffn_g, w_in, pool_w, pool_scale,
           pool_out, ret_decay, ret_norm_g, ret_out, w_out, peer_wq, peer_keys, peer_u, peer_v,
           final_g):
    b, l, d = x.shape
    n = b * l

    rows = -(-(b + 1) // SUBLANES) * SUBLANES
    cc = jnp.zeros((rows, d), F32).at[:b].set(c).at[b].set(c_ctx)
    mod = ada_stage(cc, ada_w, ada_b)
    sh1, sc1, g1, sh2, sc2, g2 = [m.reshape(b, 1, d) for m in jnp.split(mod[:b], 6, axis=-1)]
    csh1, csc1 = mod[b, :d], mod[b, d:2 * d]
    lg = jax.nn.log_sigmoid(ret_decay.astype(F32))

    w_in_bf = w_in.astype(BF16)
    s_f, s_b = ctx_stage(lg, ctx, norm_mix_g, csh1, csc1, w_in_bf[:, OFF_K:OFF_G])

    cos, sin = _rope_tables(l)
    p, q, k, v, gz, mg = proj_stage(x, norm_mix_g, sh1, sc1, cos, sin, w_in_bf)
    y = ret_stage(lg, q, k, v, s_f, s_b)
    x1, hf = mix_stage(p, y, gz, mg, x, g1, sh2, sc2, pool_w.astype(BF16), pool_scale,
                       pool_out.astype(BF16), ret_norm_g, ret_out.astype(BF16),
                       w_out.astype(BF16), norm_ffn_g)

    hf2 = hf.reshape(n, d)
    eoff, gates = route_stage(hf2, peer_wq.astype(BF16), peer_keys.astype(BF16))
    w = peer_u_stage(eoff, hf2.reshape(n, ROW_TILES, LANES), gates, _pack_table(peer_u))
    out = peer_v_stage(eoff, w, x1.reshape(n, ROW_TILES, LANES), g2.reshape(b, ROW_TILES, LANES),
                       final_g, _pack_table(peer_v), l)
    return out.reshape(b, l, d)


def kernel(x, c, ctx, c_ctx, ada_w, ada_b, norm_mix_g, norm_ffn_g, w_in, pool_w, pool_scale, pool_out, ret_decay, ret_norm_g, ret_out, w_out, peer_wq, peer_keys, peer_u, peer_v, final_g):
    assert ada_w.shape[0] == 1, "single-layer block"
    return _layer(x, ctx, c, c_ctx, ada_w[0], ada_b[0], norm_mix_g[0], norm_ffn_g[0], w_in[0],
                  pool_w[0], pool_scale[0], pool_out[0], ret_decay[0], ret_norm_g[0], ret_out[0],
                  w_out[0], peer_wq[0], peer_keys[0], peer_u[0], peer_v[0], final_g)
```

```python
import functools

import jax
import jax.numpy as jnp
import numpy as np
from jax import lax
from jax.experimental import pallas as pl
from jax.experimental.pallas import tpu as pltpu

F32 = jnp.float32
BF16 = jnp.bfloat16
I32 = jnp.int32

D_MODEL = 1024
GRID_W = 64
EPS = 1e-6

POOL_WINDOWS = (2, 4, 8, 16)
POOL_WIDTH = D_MODEL // 2
POOL_GROUP = POOL_WIDTH // len(POOL_WINDOWS)

RET_HEADS = 4
RET_DK = 128
RET_DV = 256
RET_CHUNK = 128
ROPE_BASE = 10000.0
QK_WIDTH = RET_HEADS * RET_DK
V_WIDTH = RET_HEADS * RET_DV
K_SCALE = RET_DK ** -0.5

OFF_POOL = 0
OFF_Q = OFF_POOL + POOL_WIDTH
OFF_K = OFF_Q + QK_WIDTH
OFF_V = OFF_K + QK_WIDTH
OFF_G = OFF_V + V_WIDTH
OFF_MERGE = OFF_G + V_WIDTH
IN_WIDTH = OFF_MERGE + 2 * D_MODEL

PEER_HEADS = 8
PEER_NKEYS = 128
PEER_EXPERTS = PEER_NKEYS * PEER_NKEYS
PEER_TOPK = 16
PEER_DQ = 256
PEER_SLOTS = PEER_HEADS * PEER_TOPK

LANES = 128
SUBLANES = 8
ROW_TILES = D_MODEL // LANES
PACKED_ROWS = ROW_TILES // 2
VMEM_LIMIT = 56 * 1024 * 1024

_NT = (((1,), (1,)), ((), ()))
_TN = (((0,), (0,)), ((), ()))


def _params(sem, vmem=None):
    return pltpu.CompilerParams(dimension_semantics=sem, vmem_limit_bytes=vmem)


def _resident(shape):
    nd = len(shape)
    return pl.BlockSpec(shape, lambda *_: (0,) * nd, pipeline_mode=pl.Buffered(1))


def _rms_mod(xf, g, shift, scale):
    y = xf * lax.rsqrt(jnp.mean(xf * xf, axis=-1, keepdims=True) + EPS)
    return (y * g) * (1.0 + scale) + shift


def _ada_body(c_ref, w_ref, b_ref, o_ref):
    c = c_ref[...]
    s = c * jax.nn.sigmoid(c)
    o_ref[...] = jnp.dot(s, w_ref[...], preferred_element_type=F32,
                         precision=lax.Precision.HIGHEST) + b_ref[...]


def ada_stage(cc, ada_w, ada_b, tn=512):
    r, d = cc.shape
    n = ada_w.shape[1]
    return pl.pallas_call(
        _ada_body,
        grid=(n // tn,),
        in_specs=[pl.BlockSpec((r, d), lambda j: (0, 0)),
                  pl.BlockSpec((d, tn), lambda j: (0, j)),
                  pl.BlockSpec((1, tn), lambda j: (0, j))],
        out_specs=pl.BlockSpec((r, tn), lambda j: (0, j)),
        out_shape=jax.ShapeDtypeStruct((r, n), F32),
        compiler_params=_params(("parallel",)),
        name="ada",
    )(cc, ada_w, ada_b.reshape(1, n))


def _ctx_body(lg_ref, ctx_ref, g_ref, sh_ref, sc_ref, w_ref, sf_ref, sb_ref):
    lc = ctx_ref.shape[0]
    hc = _rms_mod(ctx_ref[...], g_ref[...], sh_ref[...], sc_ref[...])
    kv = jnp.dot(hc.astype(BF16), w_ref[...], preferred_element_type=F32)
    m = lax.broadcasted_iota(I32, (lc, RET_DK), 0).astype(F32)
    for h in range(RET_HEADS):
        kf = kv[:, h * RET_DK:(h + 1) * RET_DK] * K_SCALE
        vb = kv[:, QK_WIDTH + h * RET_DV:QK_WIDTH + (h + 1) * RET_DV].astype(BF16)
        wf = jnp.exp(lg_ref[0, h] * (lc - 1.0 - m))
        wb = jnp.exp(lg_ref[1, h] * m)
        sf_ref[h] = lax.dot_general((kf * wf).astype(BF16), vb, _TN, preferred_element_type=F32)
        sb_ref[h] = lax.dot_general((kf * wb).astype(BF16), vb, _TN, preferred_element_type=F32)


def ctx_stage(lg, ctx, norm_g, csh, csc, w_kv):
    b, lc, d = ctx.shape
    vec = pl.BlockSpec((1, d), lambda i: (0, 0))
    st = jax.ShapeDtypeStruct((b, RET_HEADS, RET_DK, RET_DV), F32)
    st_spec = pl.BlockSpec((None, RET_HEADS, RET_DK, RET_DV), lambda i: (i, 0, 0, 0))
    return pl.pallas_call(
        _ctx_body,
        grid=(b,),
        in_specs=[pl.BlockSpec(memory_space=pltpu.SMEM),
                  pl.BlockSpec((None, lc, d), lambda i: (i, 0, 0)),
                  vec, vec, vec,
                  pl.BlockSpec(w_kv.shape, lambda i: (0, 0))],
        out_specs=[st_spec, st_spec],
        out_shape=[st, st],
        compiler_params=_params(("parallel",)),
        name="ctx",
    )(lg, ctx, norm_g.reshape(1, d), csh.reshape(1, d), csc.reshape(1, d), w_kv)


def _rope(a, cos, sin_signed):
    return a * cos + pltpu.roll(a, RET_DK // 2, 1) * sin_signed


def _proj_body(x_ref, g_ref, sh_ref, sc_ref, cos_ref, sin_ref, w_ref,
               p_ref, q_ref, k_ref, v_ref, gz_ref, mg_ref):
    h = _rms_mod(x_ref[...], g_ref[...], sh_ref[...], sc_ref[...]).astype(BF16)

    def mm(lo, hi):
        return jnp.dot(h, w_ref[:, lo:hi], preferred_element_type=F32)

    p_ref[...] = mm(OFF_POOL, OFF_Q)
    cos = cos_ref[...]
    sin = sin_ref[...]
    qf = mm(OFF_Q, OFF_K)
    kf = mm(OFF_K, OFF_V)
    for hd in range(RET_HEADS):
        sl = slice(hd * RET_DK, (hd + 1) * RET_DK)
        q_ref[:, sl] = _rope(qf[:, sl], cos, sin).astype(BF16)
        k_ref[:, sl] = (_rope(kf[:, sl], cos, sin) * K_SCALE).astype(BF16)
    v_ref[...] = mm(OFF_V, OFF_G).astype(BF16)
    gz_ref[...] = mm(OFF_G, OFF_MERGE).astype(BF16)
    mg_ref[...] = mm(OFF_MERGE, IN_WIDTH).astype(BF16)


def proj_stage(x, norm_g, sh1, sc1, cos, sin, w_in_bf, tm=512):
    b, l, d = x.shape
    vec_b = pl.BlockSpec((None, 1, d), lambda i, j: (i, 0, 0))
    rows = lambda w: pl.BlockSpec((None, tm, w), lambda i, j: (i, j, 0))
    tab = pl.BlockSpec((tm, RET_DK), lambda i, j: (j, 0))
    outs = [(POOL_WIDTH, F32), (QK_WIDTH, BF16), (QK_WIDTH, BF16), (V_WIDTH, BF16),
            (V_WIDTH, BF16), (2 * D_MODEL, BF16)]
    return pl.pallas_call(
        _proj_body,
        grid=(b, l // tm),
        in_specs=[rows(d), pl.BlockSpec((1, d), lambda i, j: (0, 0)), vec_b, vec_b, tab, tab,
                  _resident(w_in_bf.shape)],
        out_specs=[rows(w) for w, _ in outs],
        out_shape=[jax.ShapeDtypeStruct((b, l, w), dt) for w, dt in outs],
        compiler_params=_params(("parallel", "parallel"), VMEM_LIMIT),
        name="proj",
    )(x, norm_g.reshape(1, d), sh1, sc1, cos, sin, w_in_bf)


def _ret_body(lg_ref, q_ref, k_ref, v_ref, sf_ref, sb_ref, y_ref, sfw_scr, sbw_scr, yb_scr):
    c = RET_CHUNK
    n_chunks = q_ref.shape[0] // c
    hd = pl.program_id(1)
    lgf = lg_ref[0, hd]
    lgb = lg_ref[1, hd]
    n_i = lax.broadcasted_iota(I32, (c, c), 0)
    m_i = lax.broadcasted_iota(I32, (c, c), 1)
    rel = (n_i - m_i).astype(F32)
    intra_f = jnp.where(rel >= 0, jnp.exp(lgf * jnp.where(rel >= 0, rel, 0.0)), 0.0)
    intra_b = jnp.where(rel < 0, jnp.exp(lgb * jnp.where(rel < 0, -rel, 0.0)), 0.0)
    pos = lax.broadcasted_iota(I32, (c, RET_DK), 0).astype(F32)
    qdec_f = jnp.exp(lgf * (pos + 1.0))
    kdec_f = jnp.exp(lgf * (c - 1.0 - pos))
    qdec_b = jnp.exp(lgb * (c - pos))
    kdec_b = jnp.exp(lgb * pos)
    blk_f = jnp.exp(jnp.full((1, RET_DV), lgf * c, F32))
    blk_b = jnp.exp(jnp.full((1, RET_DV), lgb * c, F32))

    def chunk(i, st_ref, intra, qdec, kdec, blk):
        rows = pl.ds(pl.multiple_of(i * c, c), c)
        qi = q_ref[rows, :]
        ki = k_ref[rows, :]
        vi = v_ref[rows, :]
        sc = lax.dot_general(qi, ki, _NT, preferred_element_type=F32) * intra
        s = st_ref[...]
        y = (jnp.dot(sc.astype(BF16), vi, preferred_element_type=F32)
             + jnp.dot((qi.astype(F32) * qdec).astype(BF16), s.astype(BF16),
                       preferred_element_type=F32))
        st_ref[...] = s * blk + lax.dot_general((ki.astype(F32) * kdec).astype(BF16), vi, _TN,
                                                preferred_element_type=F32)
        return rows, y

    sfw_scr[...] = sf_ref[...]
    sbw_scr[...] = sb_ref[...]

    def step(i, carry):
        rows, y = chunk(i, sfw_scr, intra_f, qdec_f, kdec_f, blk_f)
        y_ref[rows, :] = y
        rows, y = chunk(n_chunks - 1 - i, sbw_scr, intra_b, qdec_b, kdec_b, blk_b)
        yb_scr[rows, :] = y
        return carry

    lax.fori_loop(0, n_chunks, step, 0)
    y_ref[...] += yb_scr[...]


def ret_stage(lg, q, k, v, s_f, s_b):
    b, l, _ = q.shape
    qk_spec = pl.BlockSpec((None, l, RET_DK), lambda i, j: (i, 0, j))
    v_spec = pl.BlockSpec((None, l, RET_DV), lambda i, j: (i, 0, j))
    st_spec = pl.BlockSpec((None, None, RET_DK, RET_DV), lambda i, j: (i, j, 0, 0))
    return pl.pallas_call(
        _ret_body,
        grid=(b, RET_HEADS),
        in_specs=[pl.BlockSpec(memory_space=pltpu.SMEM), qk_spec, qk_spec, v_spec, st_spec, st_spec],
        out_specs=v_spec,
        out_shape=jax.ShapeDtypeStruct((b, l, V_WIDTH), F32),
        scratch_shapes=[pltpu.VMEM((RET_DK, RET_DV), F32), pltpu.VMEM((RET_DK, RET_DV), F32),
                        pltpu.VMEM((l, RET_DV), F32)],
        compiler_params=_params(("parallel", "parallel")),
        name="ret",
    )(lg, q, k, v, s_f, s_b)


def _pool_bands(tm):
    r = np.arange(tm)[:, None]
    c = np.arange(tm)[None, :]
    bands = np.zeros((len(POOL_WINDOWS), 3, tm, tm), np.float32)
    for gi, w in enumerate(POOL_WINDOWS):
        lo, hi = r - w // 2, r + w - w // 2
        for j, off in enumerate((-tm, 0, tm)):
            bands[gi, j] = ((c + off >= lo) & (c + off < hi)).astype(np.float32)
    return jnp.asarray(bands, BF16)


def _mix_body(pp_ref, pm_ref, pn_ref, y_ref, gz_ref, mg_ref, x_ref, g1_ref, sh2_ref, sc2_ref,
              band_ref, pw_ref, ps_ref, po_ref, rg_ref, ro_ref, wo_ref, ng_ref,
              x1_ref, hf_ref, *, seq_len):
    tm = pm_ref.shape[0]
    li = pl.program_id(1)
    has_prev = (li > 0).astype(F32)
    has_next = (li < pl.num_programs(1) - 1).astype(F32)
    t = li * tm + lax.broadcasted_iota(I32, (tm, POOL_GROUP), 0)

    def window_sum(ref, cols, gi, j):
        pf = ref[:, cols]
        hi = pf.astype(BF16)
        lo = (pf - hi.astype(F32)).astype(BF16)
        band = band_ref[gi, j]
        return (jnp.dot(band, hi, preferred_element_type=F32)
                + jnp.dot(band, lo, preferred_element_type=F32))

    mixed = []
    for gi, w in enumerate(POOL_WINDOWS):
        cols = slice(gi * POOL_GROUP, (gi + 1) * POOL_GROUP)
        ws = (window_sum(pm_ref, cols, gi, 1) + has_prev * window_sum(pp_ref, cols, gi, 0)
              + has_next * window_sum(pn_ref, cols, gi, 2))
        cnt = (jnp.clip(t + (w - w // 2), 0, seq_len) - jnp.clip(t - w // 2, 0, seq_len)).astype(F32)
        dgi = ws / cnt - pm_ref[:, cols]
        mixed.append(jnp.dot(dgi.astype(BF16), pw_ref[gi], preferred_element_type=F32))
    mixed = jnp.concatenate(mixed, axis=1) * ps_ref[...]
    pool = jnp.dot(mixed.astype(BF16), po_ref[...], preferred_element_type=F32)

    yn = []
    for hd in range(RET_HEADS):
        yh = y_ref[:, hd * RET_DV:(hd + 1) * RET_DV]
        mu = jnp.mean(yh, axis=-1, keepdims=True)
        yc = yh - mu
        var = jnp.mean(yc * yc, axis=-1, keepdims=True)
        yn.append(yc * lax.rsqrt(var + EPS))
    yn = jnp.concatenate(yn, axis=1) * rg_ref[...]
    gate = gz_ref[...].astype(F32)
    ret = jnp.dot((yn * (gate * jax.nn.sigmoid(gate))).astype(BF16), ro_ref[...],
                  preferred_element_type=F32)

    g_pool = mg_ref[:, :D_MODEL].astype(F32)
    g_ret = mg_ref[:, D_MODEL:].astype(F32)
    merged = jax.nn.sigmoid(g_pool) * pool + jax.nn.sigmoid(g_ret) * ret
    out = jnp.dot(merged.astype(BF16), wo_ref[...], preferred_element_type=F32)
    x1 = x_ref[...] + g1_ref[...] * out
    x1_ref[...] = x1
    hf_ref[...] = _rms_mod(x1, ng_ref[...], sh2_ref[...], sc2_ref[...])


def mix_stage(p, y, gz, mg, x, g1, sh2, sc2, pool_w_bf, pool_scale, pool_out_bf, ret_norm_g,
              ret_out_bf, w_out_bf, norm_ffn_g, tm=256):
    b, l, d = x.shape
    nl = l // tm
    bands = _pool_bands(tm)
    rows = lambda w: pl.BlockSpec((None, tm, w), lambda i, j: (i, j, 0))
    vec_b = pl.BlockSpec((None, 1, d), lambda i, j: (i, 0, 0))
    const = lambda a: pl.BlockSpec(a.shape, lambda i, j: (0,) * a.ndim)
    ps = pool_scale.reshape(1, POOL_WIDTH)
    rg = ret_norm_g.reshape(1, V_WIDTH)
    ng = norm_ffn_g.reshape(1, d)
    return pl.pallas_call(
        functools.partial(_mix_body, seq_len=l),
        grid=(b, nl),
        in_specs=[pl.BlockSpec((None, tm, POOL_WIDTH), lambda i, j: (i, jnp.maximum(j - 1, 0), 0)),
                  rows(POOL_WIDTH),
                  pl.BlockSpec((None, tm, POOL_WIDTH), lambda i, j: (i, jnp.minimum(j + 1, nl - 1), 0)),
                  rows(V_WIDTH), rows(V_WIDTH), rows(2 * D_MODEL), rows(d), vec_b, vec_b, vec_b,
                  const(bands), const(pool_w_bf), const(ps), const(pool_out_bf), const(rg),
                  const(ret_out_bf), const(w_out_bf), const(ng)],
        out_specs=[rows(d), rows(d)],
        out_shape=[jax.ShapeDtypeStruct((b, l, d), F32), jax.ShapeDtypeStruct((b, l, d), F32)],
        compiler_params=_params(("parallel", "parallel"), VMEM_LIMIT),
        name="mix",
    )(p, p, p, y, gz, mg, x, g1, sh2, sc2, bands, pool_w_bf, ps, pool_out_bf, rg, ret_out_bf,
      w_out_bf, ng)


def _topk_rows(s, k):
    r, n = s.shape
    rows = lax.broadcasted_iota(I32, (r, n), 0)
    slot = lax.broadcasted_iota(I32, (k, n), 0)
    vals = jnp.zeros((k, n), F32)
    idxs = jnp.zeros((k, n), I32)
    for j in range(k):
        m = jnp.max(s, axis=0, keepdims=True)
        i = jnp.min(jnp.where(s == m, rows, r), axis=0, keepdims=True)
        vals = jnp.where(slot == j, m, vals)
        idxs = jnp.where(slot == j, i, idxs)
        s = jnp.where(rows == i, -jnp.inf, s)
    return vals, idxs


def _pick_rows(table, sel):
    out = jnp.zeros(sel.shape, table.dtype)
    for r in range(table.shape[0]):
        out = jnp.where(sel == r, table[r:r + 1, :], out)
    return out


_HALF_K = PEER_TOPK // 2
assert PEER_TOPK == 16 and SUBLANES == _HALF_K
_CAND_MID = PEER_TOPK + (_HALF_K - 1) * _HALF_K


def _candidates(s1, s2):
    parts = [s1[0:1, :] + s2]
    parts += [s1[i:i + 1, :] + s2[0:_HALF_K, :] for i in range(1, _HALF_K)]
    parts.append(s1[_HALF_K:, :] + s2[0:1, :])
    return jnp.concatenate(parts, axis=0)


def _candidate_ij(pos):
    mid = pos - PEER_TOPK
    i = jnp.where(pos < PEER_TOPK, 0, jnp.where(pos < _CAND_MID, (mid >> 3) + 1, pos - (_CAND_MID - _HALF_K)))
    j = jnp.where(pos < PEER_TOPK, pos, jnp.where(pos < _CAND_MID, mid & (_HALF_K - 1), 0))
    return i, j


def _route_body(hf_ref, wq_ref, keys_ref, e_ref, g_ref):
    tq = hf_ref.shape[0]
    half = PEER_DQ // 2
    q = jnp.dot(hf_ref[...].astype(BF16), wq_ref[...], preferred_element_type=F32).astype(BF16)
    for cb in range(tq // LANES):
        tok = slice(cb * LANES, (cb + 1) * LANES)
        e_heads, g_heads = [], []
        for hd in range(PEER_HEADS):
            sub = []
            for part in range(2):
                col = (hd * 2 + part) * half
                st = lax.dot_general(keys_ref[hd, part], q[tok, col:col + half], _NT,
                                     preferred_element_type=F32)
                sub.append(_topk_rows(st, PEER_TOPK))
            (s1, i1), (s2, i2) = sub
            best, pos = _topk_rows(_candidates(s1, s2), PEER_TOPK)
            ci, cj = _candidate_ij(pos)
            e1 = _pick_rows(i1, ci)
            e2 = _pick_rows(i2, cj)
            ex = jnp.exp(best - best[0:1, :])
            e_heads.append((e1 * PEER_NKEYS + e2) * PACKED_ROWS)
            g_heads.append(ex / jnp.sum(ex, axis=0, keepdims=True))
        e_ref[tok, :] = jnp.concatenate(e_heads, axis=0).T
        g_ref[tok, :] = jnp.concatenate(g_heads, axis=0).T


def route_stage(hf, wq_bf, keys_bf, tq=256):
    n, d = hf.shape
    rows = lambda w: pl.BlockSpec((tq, w), lambda i: (i, 0))
    return pl.pallas_call(
        _route_body,
        grid=(n // tq,),
        in_specs=[rows(d), pl.BlockSpec(wq_bf.shape, lambda i: (0, 0)),
                  pl.BlockSpec(keys_bf.shape, lambda i: (0, 0, 0, 0))],
        out_specs=[rows(PEER_SLOTS), rows(PEER_SLOTS)],
        out_shape=[jax.ShapeDtypeStruct((n, PEER_SLOTS), I32),
                   jax.ShapeDtypeStruct((n, PEER_SLOTS), F32)],
        compiler_params=_params(("parallel",), VMEM_LIMIT),
        name="route",
    )(hf, wq_bf, keys_bf)


def _pack_table(t):
    tb = t.astype(BF16).reshape(PEER_EXPERTS, PACKED_ROWS, 2, LANES).transpose(0, 1, 3, 2)
    return lax.bitcast_convert_type(tb, jnp.uint32).reshape(PEER_EXPERTS * PACKED_ROWS, LANES)


def _table_tile(words):
    return pltpu.bitcast(words, BF16).astype(F32)


OFF_BITS = 16
assert PEER_EXPERTS * PACKED_ROWS <= 1 << OFF_BITS
PEER_TB = 64
_HALF_TB = PEER_TB // 2
_HALF_WORDS = _HALF_TB * PEER_SLOTS


def _offset_copy(off_hbm, bufs, sems, step, h):
    start = (2 * step + h) * _HALF_WORDS
    return pltpu.make_async_copy(off_hbm.at[pl.ds(start, _HALF_WORDS)], bufs[h], sems.at[h])


def _for_each_half(off_hbm, bufs, sems, half_fn):
    step = pl.program_id(0)

    @pl.when(step == 0)
    def _():
        _offset_copy(off_hbm, bufs, sems, step, 0).start()

    _offset_copy(off_hbm, bufs, sems, step, 1).start()
    _offset_copy(off_hbm, bufs, sems, step, 0).wait()
    half_fn(bufs[0], 0)

    @pl.when(step + 1 < pl.num_programs(0))
    def _():
        _offset_copy(off_hbm, bufs, sems, step + 1, 0).start()

    _offset_copy(off_hbm, bufs, sems, step, 1).wait()
    half_fn(bufs[1], _HALF_TB)


def _gather_row(tbl_ref, off):
    return tbl_ref[pl.ds(pl.multiple_of(off, PACKED_ROWS), PACKED_ROWS), :]


_OFFSET_SCRATCH = [pltpu.SMEM((_HALF_WORDS,), I32), pltpu.SMEM((_HALF_WORDS,), I32),
                   pltpu.SemaphoreType.DMA((2,))]


_P_BUFS = 2


def _peer_u_body(off_hbm, h_ref, g_ref, tbl_ref, o_ref, idx_a, idx_b, sems, p_scr, s_scr):
    quad = SUBLANES // 2
    octet = quad * SUBLANES
    eye = (lax.broadcasted_iota(I32, (PEER_SLOTS, LANES), 0)
           == lax.broadcasted_iota(I32, (PEER_SLOTS, LANES), 1))
    upper = lax.broadcasted_iota(I32, (SUBLANES, LANES), 0) >= quad

    def half(idx_ref, first):
        for t in range(_HALF_TB):
            tok = first + t
            buf = tok % _P_BUFS
            hv = h_ref[tok]
            for k in range(0, PEER_SLOTS, 2):
                pa = _table_tile(_gather_row(tbl_ref, idx_ref[t * PEER_SLOTS + k])) * hv
                pb = _table_tile(_gather_row(tbl_ref, idx_ref[t * PEER_SLOTS + k + 1])) * hv
                both = jnp.where(upper, pb + pltpu.roll(pb, quad, 0), pa + pltpu.roll(pa, quad, 0))
                j, i = divmod(k, SUBLANES)
                p_scr[buf, pl.ds(j * octet + i, quad, stride=SUBLANES), :] = both[:quad]
                p_scr[buf, pl.ds(j * octet + i + 1, quad, stride=SUBLANES), :] = both[quad:]
            for j in range(PEER_SLOTS // SUBLANES):
                acc = p_scr[buf, pl.ds(j * octet, SUBLANES), :]
                for q in range(1, quad):
                    acc = acc + p_scr[buf, pl.ds(j * octet + q * SUBLANES, SUBLANES), :]
                s_scr[pl.ds(tok * PEER_SLOTS + j * SUBLANES, SUBLANES), :] = acc
        for r0 in range(first, first + _HALF_TB, SUBLANES):
            acts = []
            for i in range(SUBLANES):
                tot = jnp.sum(s_scr[pl.ds((r0 + i) * PEER_SLOTS, PEER_SLOTS), :], axis=1, keepdims=True)
                acts.append(jnp.sum(jnp.where(eye, tot, 0.0), axis=0, keepdims=True))
            act = jnp.concatenate(acts, axis=0)
            rows8 = pl.ds(r0, SUBLANES)
            o_ref[rows8, :] = g_ref[rows8, :] * (0.5 * act * (1.0 + lax.erf(act * (2.0 ** -0.5))))

    _for_each_half(off_hbm, (idx_a, idx_b), sems, half)


def peer_u_stage(eoff, hf3, gates, tbl):
    n = eoff.shape[0]
    rows = lambda: pl.BlockSpec((PEER_TB, PEER_SLOTS), lambda i: (i, 0))
    return pl.pallas_call(
        _peer_u_body,
        grid=(n // PEER_TB,),
        in_specs=[pl.BlockSpec(memory_space=pl.ANY),
                  pl.BlockSpec((PEER_TB, ROW_TILES, LANES), lambda i: (i, 0, 0)), rows(),
                  _resident(tbl.shape)],
        out_specs=rows(),
        out_shape=jax.ShapeDtypeStruct((n, PEER_SLOTS), F32),
        scratch_shapes=_OFFSET_SCRATCH + [pltpu.VMEM((_P_BUFS, PEER_SLOTS * SUBLANES // 2, LANES), F32),
                                          pltpu.VMEM((PEER_TB * PEER_SLOTS, LANES), F32)],
        compiler_params=_params(("arbitrary",), VMEM_LIMIT),
        name="peer_u",
    )(eoff.reshape(-1), hf3, gates, tbl)


_V_ACCS = 4


def _peer_v_body(off_hbm, w_ref, x_ref, g2_ref, fg_ref, tbl_ref, o_ref, idx_a, idx_b, sems):
    def half(idx_ref, first):
        for t in range(_HALF_TB):
            tok = first + t
            wrep = jnp.broadcast_to(w_ref[tok], (LANES, PEER_SLOTS)).T
            accs = [None] * _V_ACCS
            for k in range(PEER_SLOTS):
                row = _table_tile(_gather_row(tbl_ref, idx_ref[t * PEER_SLOTS + k]))
                term = jnp.broadcast_to(wrep[k:k + 1, :], (ROW_TILES, LANES)) * row
                accs[k % _V_ACCS] = term if accs[k % _V_ACCS] is None else accs[k % _V_ACCS] + term
            o_ref[tok] = (accs[0] + accs[1]) + (accs[2] + accs[3])
        rows = pl.ds(first, _HALF_TB)
        x2 = x_ref[rows] + g2_ref[...] * o_ref[rows]
        ms = jnp.sum(jnp.sum(x2 * x2, axis=2, keepdims=True), axis=1, keepdims=True) / D_MODEL
        o_ref[rows] = (x2 * lax.rsqrt(ms + EPS)) * fg_ref[...]

    _for_each_half(off_hbm, (idx_a, idx_b), sems, half)


def peer_v_stage(eoff, w, x3, g2, final_g, tbl, tokens_per_batch):
    n = eoff.shape[0]
    per_b = tokens_per_batch // PEER_TB
    tile = lambda: pl.BlockSpec((PEER_TB, ROW_TILES, LANES), lambda i: (i, 0, 0))
    return pl.pallas_call(
        _peer_v_body,
        grid=(n // PEER_TB,),
        in_specs=[pl.BlockSpec(memory_space=pl.ANY),
                  pl.BlockSpec((PEER_TB, 1, PEER_SLOTS), lambda i: (i, 0, 0)), tile(),
                  pl.BlockSpec((None, ROW_TILES, LANES), lambda i: (i // per_b, 0, 0)),
                  pl.BlockSpec((ROW_TILES, LANES), lambda i: (0, 0)),
                  _resident(tbl.shape)],
        out_specs=tile(),
        out_shape=jax.ShapeDtypeStruct((n, ROW_TILES, LANES), F32),
        scratch_shapes=list(_OFFSET_SCRATCH),
        compiler_params=_params(("arbitrary",), VMEM_LIMIT),
        name="peer_v",
    )(eoff.reshape(-1), w.reshape(n, 1, PEER_SLOTS), x3, g2, final_g.reshape(ROW_TILES, LANES), tbl)


def _rope_tables(l):
    quarter = RET_DK // 4
    rows = l // GRID_W
    row = jnp.repeat(jnp.arange(rows, dtype=F32), GRID_W)
    col = jnp.tile(jnp.arange(GRID_W, dtype=F32), rows)
    inv = ROPE_BASE ** (-jnp.arange(quarter, dtype=F32) / quarter)
    ang = jnp.concatenate([row[:, None] * inv, col[:, None] * inv], axis=-1)
    cos, sin = jnp.cos(ang), jnp.sin(ang)
    return jnp.concatenate([cos, cos], axis=-1), jnp.concatenate([-sin, sin], axis=-1)


def _layer(x, ctx, c, c_ctx, ada_w, ada_b, norm_mix_g, norm_ffn_g, w_in, pool_w, pool_scale,
           pool_out, ret_decay, ret_norm_g, ret_out, w_out, peer_wq, peer_keys, peer_u, peer_v,
           final_g):
    b, l, d = x.shape
    n = b * l

    rows = -(-(b + 1) // SUBLANES) * SUBLANES
    cc = jnp.zeros((rows, d), F32).at[:b].set(c).at[b].set(c_ctx)
    mod = ada_stage(cc, ada_w, ada_b)
    sh1, sc1, g1, sh2, sc2, g2 = [m.reshape(b, 1, d) for m in jnp.split(mod[:b], 6, axis=-1)]
    csh1, csc1 = mod[b, :d], mod[b, d:2 * d]
    lg = jax.nn.log_sigmoid(ret_decay.astype(F32))

    w_in_bf = w_in.astype(BF16)
    s_f, s_b = ctx_stage(lg, ctx, norm_mix_g, csh1, csc1, w_in_bf[:, OFF_K:OFF_G])

    cos, sin = _rope_tables(l)
    p, q, k, v, gz, mg = proj_stage(x, norm_mix_g, sh1, sc1, cos, sin, w_in_bf)
    y = ret_stage(lg, q, k, v, s_f, s_b)
    x1, hf = mix_stage(p, y, gz, mg, x, g1, sh2, sc2, pool_w.astype(BF16), pool_scale,
                       pool_out.astype(BF16), ret_norm_g, ret_out.astype(BF16),
                       w_out.astype(BF16), norm_ffn_g)

    hf2 = hf.reshape(n, d)
    eoff, gates = route_stage(hf2, peer_wq.astype(BF16), peer_keys.astype(BF16))
    w = peer_u_stage(eoff, hf2.reshape(n, ROW_TILES, LANES), gates, _pack_table(peer_u))
    out = peer_v_stage(eoff, w, x1.reshape(n, ROW_TILES, LANES), g2.reshape(b, ROW_TILES, LANES),
                       final_g, _pack_table(peer_v), l)
    return out.reshape(b, l, d)


def kernel(x, c, ctx, c_ctx, ada_w, ada_b, norm_mix_g, norm_ffn_g, w_in, pool_w, pool_scale, pool_out, ret_decay, ret_norm_g, ret_out, w_out, peer_wq, peer_keys, peer_u, peer_v, final_g):
    assert ada_w.shape[0] == 1, "single-layer block"
    return _layer(x, ctx, c, c_ctx, ada_w[0], ada_b[0], norm_mix_g[0], norm_ffn_g[0], w_in[0],
                  pool_w[0], pool_scale[0], pool_out[0], ret_decay[0], ret_norm_g[0], ret_out[0],
                  w_out[0], peer_wq[0], peer_keys[0], peer_u[0], peer_v[0], final_g)
```

```python
import functools

import jax
import jax.numpy as jnp
import numpy as np
from jax import lax
from jax.experimental import pallas as pl
from jax.experimental.pallas import tpu as pltpu

F32 = jnp.float32
BF16 = jnp.bfloat16
I32 = jnp.int32

D_MODEL = 1024
GRID_W = 64
EPS = 1e-6

POOL_WINDOWS = (2, 4, 8, 16)
POOL_WIDTH = D_MODEL // 2
POOL_GROUP = POOL_WIDTH // len(POOL_WINDOWS)

RET_HEADS = 4
RET_DK = 128
RET_DV = 256
RET_CHUNK = 128
ROPE_BASE = 10000.0
QK_WIDTH = RET_HEADS * RET_DK
V_WIDTH = RET_HEADS * RET_DV
K_SCALE = RET_DK ** -0.5

OFF_POOL = 0
OFF_Q = OFF_POOL + POOL_WIDTH
OFF_K = OFF_Q + QK_WIDTH
OFF_V = OFF_K + QK_WIDTH
OFF_G = OFF_V + V_WIDTH
OFF_MERGE = OFF_G + V_WIDTH
IN_WIDTH = OFF_MERGE + 2 * D_MODEL

PEER_HEADS = 8
PEER_NKEYS = 128
PEER_EXPERTS = PEER_NKEYS * PEER_NKEYS
PEER_TOPK = 16
PEER_DQ = 256
PEER_SLOTS = PEER_HEADS * PEER_TOPK

LANES = 128
SUBLANES = 8
ROW_TILES = D_MODEL // LANES
PACKED_ROWS = ROW_TILES // 2
VMEM_LIMIT = 56 * 1024 * 1024

_NT = (((1,), (1,)), ((), ()))
_TN = (((0,), (0,)), ((), ()))


def _params(sem, vmem=None):
    return pltpu.CompilerParams(dimension_semantics=sem, vmem_limit_bytes=vmem)


def _resident(shape):
    nd = len(shape)
    return pl.BlockSpec(shape, lambda *_: (0,) * nd, pipeline_mode=pl.Buffered(1))


def _rms_mod(xf, g, shift, scale):
    y = xf * lax.rsqrt(jnp.mean(xf * xf, axis=-1, keepdims=True) + EPS)
    return (y * g) * (1.0 + scale) + shift


def _ada_body(c_ref, w_ref, b_ref, o_ref):
    c = c_ref[...]
    s = c * jax.nn.sigmoid(c)
    o_ref[...] = jnp.dot(s, w_ref[...], preferred_element_type=F32,
                         precision=lax.Precision.HIGHEST) + b_ref[...]


def ada_stage(cc, ada_w, ada_b, tn=512):
    r, d = cc.shape
    n = ada_w.shape[1]
    return pl.pallas_call(
        _ada_body,
        grid=(n // tn,),
        in_specs=[pl.BlockSpec((r, d), lambda j: (0, 0)),
                  pl.BlockSpec((d, tn), lambda j: (0, j)),
                  pl.BlockSpec((1, tn), lambda j: (0, j))],
        out_specs=pl.BlockSpec((r, tn), lambda j: (0, j)),
        out_shape=jax.ShapeDtypeStruct((r, n), F32),
        compiler_params=_params(("parallel",)),
        name="ada",
    )(cc, ada_w, ada_b.reshape(1, n))


def _ctx_body(lg_ref, ctx_ref, g_ref, sh_ref, sc_ref, w_ref, sf_ref, sb_ref):
    lc = ctx_ref.shape[0]
    hc = _rms_mod(ctx_ref[...], g_ref[...], sh_ref[...], sc_ref[...])
    kv = jnp.dot(hc.astype(BF16), w_ref[...], preferred_element_type=F32)
    m = lax.broadcasted_iota(I32, (lc, RET_DK), 0).astype(F32)
    for h in range(RET_HEADS):
        kf = kv[:, h * RET_DK:(h + 1) * RET_DK] * K_SCALE
        vb = kv[:, QK_WIDTH + h * RET_DV:QK_WIDTH + (h + 1) * RET_DV].astype(BF16)
        wf = jnp.exp(lg_ref[0, h] * (lc - 1.0 - m))
        wb = jnp.exp(lg_ref[1, h] * m)
        sf_ref[h] = lax.dot_general((kf * wf).astype(BF16), vb, _TN, preferred_element_type=F32)
        sb_ref[h] = lax.dot_general((kf * wb).astype(BF16), vb, _TN, preferred_element_type=F32)


def ctx_stage(lg, ctx, norm_g, csh, csc, w_kv):
    b, lc, d = ctx.shape
    vec = pl.BlockSpec((1, d), lambda i: (0, 0))
    st = jax.ShapeDtypeStruct((b, RET_HEADS, RET_DK, RET_DV), F32)
    st_spec = pl.BlockSpec((None, RET_HEADS, RET_DK, RET_DV), lambda i: (i, 0, 0, 0))
    return pl.pallas_call(
        _ctx_body,
        grid=(b,),
        in_specs=[pl.BlockSpec(memory_space=pltpu.SMEM),
                  pl.BlockSpec((None, lc, d), lambda i: (i, 0, 0)),
                  vec, vec, vec,
                  pl.BlockSpec(w_kv.shape, lambda i: (0, 0))],
        out_specs=[st_spec, st_spec],
        out_shape=[st, st],
        compiler_params=_params(("parallel",)),
        name="ctx",
    )(lg, ctx, norm_g.reshape(1, d), csh.reshape(1, d), csc.reshape(1, d), w_kv)


def _rope(a, cos, sin_signed):
    return a * cos + pltpu.roll(a, RET_DK // 2, 1) * sin_signed


def _proj_body(x_ref, g_ref, sh_ref, sc_ref, cos_ref, sin_ref, w_ref,
               p_ref, q_ref, k_ref, v_ref, gz_ref, mg_ref):
    h = _rms_mod(x_ref[...], g_ref[...], sh_ref[...], sc_ref[...]).astype(BF16)

    def mm(lo, hi):
        return jnp.dot(h, w_ref[:, lo:hi], preferred_element_type=F32)

    p_ref[...] = mm(OFF_POOL, OFF_Q)
    cos = cos_ref[...]
    sin = sin_ref[...]
    qf = mm(OFF_Q, OFF_K)
    kf = mm(OFF_K, OFF_V)
    for hd in range(RET_HEADS):
        sl = slice(hd * RET_DK, (hd + 1) * RET_DK)
        q_ref[:, sl] = _rope(qf[:, sl], cos, sin).astype(BF16)
        k_ref[:, sl] = (_rope(kf[:, sl], cos, sin) * K_SCALE).astype(BF16)
    v_ref[...] = mm(OFF_V, OFF_G).astype(BF16)
    gz_ref[...] = mm(OFF_G, OFF_MERGE).astype(BF16)
    mg_ref[...] = mm(OFF_MERGE, IN_WIDTH).astype(BF16)


def proj_stage(x, norm_g, sh1, sc1, cos, sin, w_in_bf, tm=512):
    b, l, d = x.shape
    vec_b = pl.BlockSpec((None, 1, d), lambda i, j: (i, 0, 0))
    rows = lambda w: pl.BlockSpec((None, tm, w), lambda i, j: (i, j, 0))
    tab = pl.BlockSpec((tm, RET_DK), lambda i, j: (j, 0))
    outs = [(POOL_WIDTH, F32), (QK_WIDTH, BF16), (QK_WIDTH, BF16), (V_WIDTH, BF16),
            (V_WIDTH, BF16), (2 * D_MODEL, BF16)]
    return pl.pallas_call(
        _proj_body,
        grid=(b, l // tm),
        in_specs=[rows(d), pl.BlockSpec((1, d), lambda i, j: (0, 0)), vec_b, vec_b, tab, tab,
                  _resident(w_in_bf.shape)],
        out_specs=[rows(w) for w, _ in outs],
        out_shape=[jax.ShapeDtypeStruct((b, l, w), dt) for w, dt in outs],
        compiler_params=_params(("parallel", "parallel"), VMEM_LIMIT),
        name="proj",
    )(x, norm_g.reshape(1, d), sh1, sc1, cos, sin, w_in_bf)


def _ret_body(lg_ref, q_ref, k_ref, v_ref, sf_ref, sb_ref, y_ref, sfw_scr, sbw_scr, yb_scr):
    c = RET_CHUNK
    n_chunks = q_ref.shape[0] // c
    hd = pl.program_id(1)
    lgf = lg_ref[0, hd]
    lgb = lg_ref[1, hd]
    n_i = lax.broadcasted_iota(I32, (c, c), 0)
    m_i = lax.broadcasted_iota(I32, (c, c), 1)
    rel = (n_i - m_i).astype(F32)
    intra_f = jnp.where(rel >= 0, jnp.exp(lgf * jnp.where(rel >= 0, rel, 0.0)), 0.0)
    intra_b = jnp.where(rel < 0, jnp.exp(lgb * jnp.where(rel < 0, -rel, 0.0)), 0.0)
    pos = lax.broadcasted_iota(I32, (c, RET_DK), 0).astype(F32)
    qdec_f = jnp.exp(lgf * (pos + 1.0))
    kdec_f = jnp.exp(lgf * (c - 1.0 - pos))
    qdec_b = jnp.exp(lgb * (c - pos))
    kdec_b = jnp.exp(lgb * pos)
    blk_f = jnp.exp(jnp.full((1, RET_DV), lgf * c, F32))
    blk_b = jnp.exp(jnp.full((1, RET_DV), lgb * c, F32))

    def chunk(i, st_ref, intra, qdec, kdec, blk):
        rows = pl.ds(pl.multiple_of(i * c, c), c)
        qi = q_ref[rows, :]
        ki = k_ref[rows, :]
        vi = v_ref[rows, :]
        sc = lax.dot_general(qi, ki, _NT, preferred_element_type=F32) * intra
        s = st_ref[...]
        y = (jnp.dot(sc.astype(BF16), vi, preferred_element_type=F32)
             + jnp.dot((qi.astype(F32) * qdec).astype(BF16), s.astype(BF16),
                       preferred_element_type=F32))
        st_ref[...] = s * blk + lax.dot_general((ki.astype(F32) * kdec).astype(BF16), vi, _TN,
                                                preferred_element_type=F32)
        return rows, y

    sfw_scr[...] = sf_ref[...]
    sbw_scr[...] = sb_ref[...]

    def step(i, carry):
        rows, y = chunk(i, sfw_scr, intra_f, qdec_f, kdec_f, blk_f)
        y_ref[rows, :] = y
        rows, y = chunk(n_chunks - 1 - i, sbw_scr, intra_b, qdec_b, kdec_b, blk_b)
        yb_scr[rows, :] = y
        return carry

    lax.fori_loop(0, n_chunks, step, 0)
    y_ref[...] += yb_scr[...]


def ret_stage(lg, q, k, v, s_f, s_b):
    b, l, _ = q.shape
    qk_spec = pl.BlockSpec((None, l, RET_DK), lambda i, j: (i, 0, j))
    v_spec = pl.BlockSpec((None, l, RET_DV), lambda i, j: (i, 0, j))
    st_spec = pl.BlockSpec((None, None, RET_DK, RET_DV), lambda i, j: (i, j, 0, 0))
    return pl.pallas_call(
        _ret_body,
        grid=(b, RET_HEADS),
        in_specs=[pl.BlockSpec(memory_space=pltpu.SMEM), qk_spec, qk_spec, v_spec, st_spec, st_spec],
        out_specs=v_spec,
        out_shape=jax.ShapeDtypeStruct((b, l, V_WIDTH), F32),
        scratch_shapes=[pltpu.VMEM((RET_DK, RET_DV), F32), pltpu.VMEM((RET_DK, RET_DV), F32),
                        pltpu.VMEM((l, RET_DV), F32)],
        compiler_params=_params(("parallel", "parallel")),
        name="ret",
    )(lg, q, k, v, s_f, s_b)


def _pool_bands(tm):
    r = np.arange(tm)[:, None]
    c = np.arange(tm)[None, :]
    bands = np.zeros((len(POOL_WINDOWS), 3, tm, tm), np.float32)
    for gi, w in enumerate(POOL_WINDOWS):
        lo, hi = r - w // 2, r + w - w // 2
        for j, off in enumerate((-tm, 0, tm)):
            bands[gi, j] = ((c + off >= lo) & (c + off < hi)).astype(np.float32)
    return jnp.asarray(bands, BF16)


def _mix_body(pp_ref, pm_ref, pn_ref, y_ref, gz_ref, mg_ref, x_ref, g1_ref, sh2_ref, sc2_ref,
              band_ref, pw_ref, ps_ref, po_ref, rg_ref, ro_ref, wo_ref, ng_ref,
              x1_ref, hf_ref, *, seq_len):
    tm = pm_ref.shape[0]
    li = pl.program_id(1)
    has_prev = (li > 0).astype(F32)
    has_next = (li < pl.num_programs(1) - 1).astype(F32)
    t = li * tm + lax.broadcasted_iota(I32, (tm, POOL_GROUP), 0)

    def window_sum(ref, cols, gi, j):
        pf = ref[:, cols]
        hi = pf.astype(BF16)
        lo = (pf - hi.astype(F32)).astype(BF16)
        band = band_ref[gi, j]
        return (jnp.dot(band, hi, preferred_element_type=F32)
                + jnp.dot(band, lo, preferred_element_type=F32))

    mixed = []
    for gi, w in enumerate(POOL_WINDOWS):
        cols = slice(gi * POOL_GROUP, (gi + 1) * POOL_GROUP)
        ws = (window_sum(pm_ref, cols, gi, 1) + has_prev * window_sum(pp_ref, cols, gi, 0)
              + has_next * window_sum(pn_ref, cols, gi, 2))
        cnt = (jnp.clip(t + (w - w // 2), 0, seq_len) - jnp.clip(t - w // 2, 0, seq_len)).astype(F32)
        dgi = ws / cnt - pm_ref[:, cols]
        mixed.append(jnp.dot(dgi.astype(BF16), pw_ref[gi], preferred_element_type=F32))
    mixed = jnp.concatenate(mixed, axis=1) * ps_ref[...]
    pool = jnp.dot(mixed.astype(BF16), po_ref[...], preferred_element_type=F32)

    yn = []
    for hd in range(RET_HEADS):
        yh = y_ref[:, hd * RET_DV:(hd + 1) * RET_DV]
        mu = jnp.mean(yh, axis=-1, keepdims=True)
        yc = yh - mu
        var = jnp.mean(yc * yc, axis=-1, keepdims=True)
        yn.append(yc * lax.rsqrt(var + EPS))
    yn = jnp.concatenate(yn, axis=1) * rg_ref[...]
    gate = gz_ref[...].astype(F32)
    ret = jnp.dot((yn * (gate * jax.nn.sigmoid(gate))).astype(BF16), ro_ref[...],
                  preferred_element_type=F32)

    g_pool = mg_ref[:, :D_MODEL].astype(F32)
    g_ret = mg_ref[:, D_MODEL:].astype(F32)
    merged = jax.nn.sigmoid(g_pool) * pool + jax.nn.sigmoid(g_ret) * ret
    out = jnp.dot(merged.astype(BF16), wo_ref[...], preferred_element_type=F32)
    x1 = x_ref[...] + g1_ref[...] * out
    x1_ref[...] = x1
    hf_ref[...] = _rms_mod(x1, ng_ref[...], sh2_ref[...], sc2_ref[...])


def mix_stage(p, y, gz, mg, x, g1, sh2, sc2, pool_w_bf, pool_scale, pool_out_bf, ret_norm_g,
              ret_out_bf, w_out_bf, norm_ffn_g, tm=256):
    b, l, d = x.shape
    nl = l // tm
    bands = _pool_bands(tm)
    rows = lambda w: pl.BlockSpec((None, tm, w), lambda i, j: (i, j, 0))
    vec_b = pl.BlockSpec((None, 1, d), lambda i, j: (i, 0, 0))
    const = lambda a: pl.BlockSpec(a.shape, lambda i, j: (0,) * a.ndim)
    ps = pool_scale.reshape(1, POOL_WIDTH)
    rg = ret_norm_g.reshape(1, V_WIDTH)
    ng = norm_ffn_g.reshape(1, d)
    return pl.pallas_call(
        functools.partial(_mix_body, seq_len=l),
        grid=(b, nl),
        in_specs=[pl.BlockSpec((None, tm, POOL_WIDTH), lambda i, j: (i, jnp.maximum(j - 1, 0), 0)),
                  rows(POOL_WIDTH),
                  pl.BlockSpec((None, tm, POOL_WIDTH), lambda i, j: (i, jnp.minimum(j + 1, nl - 1), 0)),
                  rows(V_WIDTH), rows(V_WIDTH), rows(2 * D_MODEL), rows(d), vec_b, vec_b, vec_b,
                  const(bands), const(pool_w_bf), const(ps), const(pool_out_bf), const(rg),
                  const(ret_out_bf), const(w_out_bf), const(ng)],
        out_specs=[rows(d), rows(d)],
        out_shape=[jax.ShapeDtypeStruct((b, l, d), F32), jax.ShapeDtypeStruct((b, l, d), F32)],
        compiler_params=_params(("parallel", "parallel"), VMEM_LIMIT),
        name="mix",
    )(p, p, p, y, gz, mg, x, g1, sh2, sc2, bands, pool_w_bf, ps, pool_out_bf, rg, ret_out_bf,
      w_out_bf, ng)


def _topk_rows(s, k):
    r, n = s.shape
    rows = lax.broadcasted_iota(I32, (r, n), 0)
    slot = lax.broadcasted_iota(I32, (k, n), 0)
    vals = jnp.zeros((k, n), F32)
    idxs = jnp.zeros((k, n), I32)
    for j in range(k):
        m = jnp.max(s, axis=0, keepdims=True)
        i = jnp.min(jnp.where(s == m, rows, r), axis=0, keepdims=True)
        vals = jnp.where(slot == j, m, vals)
        idxs = jnp.where(slot == j, i, idxs)
        s = jnp.where(rows == i, -jnp.inf, s)
    return vals, idxs


def _pick_rows(table, sel):
    out = jnp.zeros(sel.shape, table.dtype)
    for r in range(table.shape[0]):
        out = jnp.where(sel == r, table[r:r + 1, :], out)
    return out


_HALF_K = PEER_TOPK // 2
assert PEER_TOPK == 16 and SUBLANES == _HALF_K
_CAND_MID = PEER_TOPK + (_HALF_K - 1) * _HALF_K


def _candidates(s1, s2):
    parts = [s1[0:1, :] + s2]
    parts += [s1[i:i + 1, :] + s2[0:_HALF_K, :] for i in range(1, _HALF_K)]
    parts.append(s1[_HALF_K:, :] + s2[0:1, :])
    return jnp.concatenate(parts, axis=0)


def _candidate_ij(pos):
    mid = pos - PEER_TOPK
    i = jnp.where(pos < PEER_TOPK, 0, jnp.where(pos < _CAND_MID, (mid >> 3) + 1, pos - (_CAND_MID - _HALF_K)))
    j = jnp.where(pos < PEER_TOPK, pos, jnp.where(pos < _CAND_MID, mid & (_HALF_K - 1), 0))
    return i, j


def _route_body(hf_ref, wq_ref, keys_ref, e_ref, g_ref):
    tq = hf_ref.shape[0]
    half = PEER_DQ // 2
    q = jnp.dot(hf_ref[...].astype(BF16), wq_ref[...], preferred_element_type=F32).astype(BF16)
    for cb in range(tq // LANES):
        tok = slice(cb * LANES, (cb + 1) * LANES)
        e_heads, g_heads = [], []
        for hd in range(PEER_HEADS):
            sub = []
            for part in range(2):
                col = (hd * 2 + part) * half
                st = lax.dot_general(keys_ref[hd, part], q[tok, col:col + half], _NT,
                                     preferred_element_type=F32)
                sub.append(_topk_rows(st, PEER_TOPK))
            (s1, i1), (s2, i2) = sub
            best, pos = _topk_rows(_candidates(s1, s2), PEER_TOPK)
            ci, cj = _candidate_ij(pos)
            e1 = _pick_rows(i1, ci)
            e2 = _pick_rows(i2, cj)
            ex = jnp.exp(best - best[0:1, :])
            e_heads.append((e1 * PEER_NKEYS + e2) * PACKED_ROWS)
            g_heads.append(ex / jnp.sum(ex, axis=0, keepdims=True))
        e_ref[tok, :] = jnp.concatenate(e_heads, axis=0).T
        g_ref[tok, :] = jnp.concatenate(g_heads, axis=0).T


def route_stage(hf, wq_bf, keys_bf, tq=256):
    n, d = hf.shape
    rows = lambda w: pl.BlockSpec((tq, w), lambda i: (i, 0))
    return pl.pallas_call(
        _route_body,
        grid=(n // tq,),
        in_specs=[rows(d), pl.BlockSpec(wq_bf.shape, lambda i: (0, 0)),
                  pl.BlockSpec(keys_bf.shape, lambda i: (0, 0, 0, 0))],
        out_specs=[rows(PEER_SLOTS), rows(PEER_SLOTS)],
        out_shape=[jax.ShapeDtypeStruct((n, PEER_SLOTS), I32),
                   jax.ShapeDtypeStruct((n, PEER_SLOTS), F32)],
        compiler_params=_params(("parallel",), VMEM_LIMIT),
        name="route",
    )(hf, wq_bf, keys_bf)


def _pack_table(t):
    tb = t.astype(BF16).reshape(PEER_EXPERTS, PACKED_ROWS, 2, LANES).transpose(0, 1, 3, 2)
    return lax.bitcast_convert_type(tb, jnp.uint32).reshape(PEER_EXPERTS * PACKED_ROWS, LANES)


def _table_tile(words):
    return pltpu.bitcast(words, BF16).astype(F32)


OFF_BITS = 16
assert PEER_EXPERTS * PACKED_ROWS <= 1 << OFF_BITS
PEER_TB = 64
_HALF_TB = PEER_TB // 2
_HALF_WORDS = _HALF_TB * PEER_SLOTS


def _offset_copy(off_hbm, bufs, sems, step, h):
    start = (2 * step + h) * _HALF_WORDS
    return pltpu.make_async_copy(off_hbm.at[pl.ds(start, _HALF_WORDS)], bufs[h], sems.at[h])


def _for_each_half(off_hbm, bufs, sems, half_fn):
    step = pl.program_id(0)

    @pl.when(step == 0)
    def _():
        _offset_copy(off_hbm, bufs, sems, step, 0).start()

    _offset_copy(off_hbm, bufs, sems, step, 1).start()
    _offset_copy(off_hbm, bufs, sems, step, 0).wait()
    half_fn(bufs[0], 0)

    @pl.when(step + 1 < pl.num_programs(0))
    def _():
        _offset_copy(off_hbm, bufs, sems, step + 1, 0).start()

    _offset_copy(off_hbm, bufs, sems, step, 1).wait()
    half_fn(bufs[1], _HALF_TB)


def _gather_row(tbl_ref, off):
    return tbl_ref[pl.ds(pl.multiple_of(off, PACKED_ROWS), PACKED_ROWS), :]


_OFFSET_SCRATCH = [pltpu.SMEM((_HALF_WORDS,), I32), pltpu.SMEM((_HALF_WORDS,), I32),
                   pltpu.SemaphoreType.DMA((2,))]


_P_BUFS = 2


def _peer_u_body(off_hbm, h_ref, g_ref, tbl_ref, o_ref, idx_a, idx_b, sems, p_scr, s_scr):
    quad = SUBLANES // 2
    octet = quad * SUBLANES
    eye = (lax.broadcasted_iota(I32, (PEER_SLOTS, LANES), 0)
           == lax.broadcasted_iota(I32, (PEER_SLOTS, LANES), 1))
    upper = lax.broadcasted_iota(I32, (SUBLANES, LANES), 0) >= quad
    sel = (lax.broadcasted_iota(I32, (SUBLANES, SUBLANES * SUBLANES), 1) // SUBLANES
           == lax.broadcasted_iota(I32, (SUBLANES, SUBLANES * SUBLANES), 0)).astype(BF16)

    def half(idx_ref, first):
        for t in range(_HALF_TB):
            tok = first + t
            buf = tok % _P_BUFS
            hv = h_ref[tok]
            for j in range(PEER_SLOTS // SUBLANES):
                prods = [_table_tile(_gather_row(tbl_ref, idx_ref[t * PEER_SLOTS + k])) * hv
                         for k in range(j * SUBLANES, (j + 1) * SUBLANES)]
                stack = jnp.concatenate(prods, axis=0)
                acc = jnp.dot(sel, stack.astype(BF16), preferred_element_type=F32)
                s_scr[pl.ds(tok * PEER_SLOTS + j * SUBLANES, SUBLANES), :] = acc
        for r0 in range(first, first + _HALF_TB, SUBLANES):
            acts = []
            for i in range(SUBLANES):
                tot = jnp.sum(s_scr[pl.ds((r0 + i) * PEER_SLOTS, PEER_SLOTS), :], axis=1, keepdims=True)
                acts.append(jnp.sum(jnp.where(eye, tot, 0.0), axis=0, keepdims=True))
            act = jnp.concatenate(acts, axis=0)
            rows8 = pl.ds(r0, SUBLANES)
            o_ref[rows8, :] = g_ref[rows8, :] * (0.5 * act * (1.0 + lax.erf(act * (2.0 ** -0.5))))

    _for_each_half(off_hbm, (idx_a, idx_b), sems, half)


def peer_u_stage(eoff, hf3, gates, tbl):
    n = eoff.shape[0]
    rows = lambda: pl.BlockSpec((PEER_TB, PEER_SLOTS), lambda i: (i, 0))
    return pl.pallas_call(
        _peer_u_body,
        grid=(n // PEER_TB,),
        in_specs=[pl.BlockSpec(memory_space=pl.ANY),
                  pl.BlockSpec((PEER_TB, ROW_TILES, LANES), lambda i: (i, 0, 0)), rows(),
                  _resident(tbl.shape)],
        out_specs=rows(),
        out_shape=jax.ShapeDtypeStruct((n, PEER_SLOTS), F32),
        scratch_shapes=_OFFSET_SCRATCH + [pltpu.VMEM((_P_BUFS, PEER_SLOTS * SUBLANES // 2, LANES), F32),
                                          pltpu.VMEM((PEER_TB * PEER_SLOTS, LANES), F32)],
        compiler_params=_params(("arbitrary",), VMEM_LIMIT),
        name="peer_u",
    )(eoff.reshape(-1), hf3, gates, tbl)


_V_ACCS = 4


def _peer_v_body(off_hbm, w_ref, x_ref, g2_ref, fg_ref, tbl_ref, o_ref, idx_a, idx_b, sems):
    def half(idx_ref, first):
        for t in range(_HALF_TB):
            tok = first + t
            wrep = jnp.broadcast_to(w_ref[tok], (LANES, PEER_SLOTS)).T
            accs = [None] * _V_ACCS
            for k in range(PEER_SLOTS):
                row = _table_tile(_gather_row(tbl_ref, idx_ref[t * PEER_SLOTS + k]))
                term = jnp.broadcast_to(wrep[k:k + 1, :], (ROW_TILES, LANES)) * row
                accs[k % _V_ACCS] = term if accs[k % _V_ACCS] is None else accs[k % _V_ACCS] + term
            o_ref[tok] = (accs[0] + accs[1]) + (accs[2] + accs[3])
        rows = pl.ds(first, _HALF_TB)
        x2 = x_ref[rows] + g2_ref[...] * o_ref[rows]
        ms = jnp.sum(jnp.sum(x2 * x2, axis=2, keepdims=True), axis=1, keepdims=True) / D_MODEL
        o_ref[rows] = (x2 * lax.rsqrt(ms + EPS)) * fg_ref[...]

    _for_each_half(off_hbm, (idx_a, idx_b), sems, half)


def peer_v_stage(eoff, w, x3, g2, final_g, tbl, tokens_per_batch):
    n = eoff.shape[0]
    per_b = tokens_per_batch // PEER_TB
    tile = lambda: pl.BlockSpec((PEER_TB, ROW_TILES, LANES), lambda i: (i, 0, 0))
    return pl.pallas_call(
        _peer_v_body,
        grid=(n // PEER_TB,),
        in_specs=[pl.BlockSpec(memory_space=pl.ANY),
                  pl.BlockSpec((PEER_TB, 1, PEER_SLOTS), lambda i: (i, 0, 0)), tile(),
                  pl.BlockSpec((None, ROW_TILES, LANES), lambda i: (i // per_b, 0, 0)),
                  pl.BlockSpec((ROW_TILES, LANES), lambda i: (0, 0)),
                  _resident(tbl.shape)],
        out_specs=tile(),
        out_shape=jax.ShapeDtypeStruct((n, ROW_TILES, LANES), F32),
        scratch_shapes=list(_OFFSET_SCRATCH),
        compiler_params=_params(("arbitrary",), VMEM_LIMIT),
        name="peer_v",
    )(eoff.reshape(-1), w.reshape(n, 1, PEER_SLOTS), x3, g2, final_g.reshape(ROW_TILES, LANES), tbl)


def _rope_tables(l):
    quarter = RET_DK // 4
    rows = l // GRID_W
    row = jnp.repeat(jnp.arange(rows, dtype=F32), GRID_W)
    col = jnp.tile(jnp.arange(GRID_W, dtype=F32), rows)
    inv = ROPE_BASE ** (-jnp.arange(quarter, dtype=F32) / quarter)
    ang = jnp.concatenate([row[:, None] * inv, col[:, None] * inv], axis=-1)
    cos, sin = jnp.cos(ang), jnp.sin(ang)
    return jnp.concatenate([cos, cos], axis=-1), jnp.concatenate([-sin, sin], axis=-1)


def _layer(x, ctx, c, c_ctx, ada_w, ada_b, norm_mix_g, norm_ffn_g, w_in, pool_w, pool_scale,
           pool_out, ret_decay, ret_norm_g, ret_out, w_out, peer_wq, peer_keys, peer_u, peer_v,
           final_g):
    b, l, d = x.shape
    n = b * l

    rows = -(-(b + 1) // SUBLANES) * SUBLANES
    cc = jnp.zeros((rows, d), F32).at[:b].set(c).at[b].set(c_ctx)
    mod = ada_stage(cc, ada_w, ada_b)
    sh1, sc1, g1, sh2, sc2, g2 = [m.reshape(b, 1, d) for m in jnp.split(mod[:b], 6, axis=-1)]
    csh1, csc1 = mod[b, :d], mod[b, d:2 * d]
    lg = jax.nn.log_sigmoid(ret_decay.astype(F32))

    w_in_bf = w_in.astype(BF16)
    s_f, s_b = ctx_stage(lg, ctx, norm_mix_g, csh1, csc1, w_in_bf[:, OFF_K:OFF_G])

    cos, sin = _rope_tables(l)
    p, q, k, v, gz, mg = proj_stage(x, norm_mix_g, sh1, sc1, cos, sin, w_in_bf)
    y = ret_stage(lg, q, k, v, s_f, s_b)
    x1, hf = mix_stage(p, y, gz, mg, x, g1, sh2, sc2, pool_w.astype(BF16), pool_scale,
                       pool_out.astype(BF16), ret_norm_g, ret_out.astype(BF16),
                       w_out.astype(BF16), norm_ffn_g)

    hf2 = hf.reshape(n, d)
    eoff, gates = route_stage(hf2, peer_wq.astype(BF16), peer_keys.astype(BF16))
    w = peer_u_stage(eoff, hf2.reshape(n, ROW_TILES, LANES), gates, _pack_table(peer_u))
    out = peer_v_stage(eoff, w, x1.reshape(n, ROW_TILES, LANES), g2.reshape(b, ROW_TILES, LANES),
                       final_g, _pack_table(peer_v), l)
    return out.reshape(b, l, d)


def kernel(x, c, ctx, c_ctx, ada_w, ada_b, norm_mix_g, norm_ffn_g, w_in, pool_w, pool_scale, pool_out, ret_decay, ret_norm_g, ret_out, w_out, peer_wq, peer_keys, peer_u, peer_v, final_g):
    assert ada_w.shape[0] == 1, "single-layer block"
    return _layer(x, ctx, c, c_ctx, ada_w[0], ada_b[0], norm_mix_g[0], norm_ffn_g[0], w_in[0],
                  pool_w[0], pool_scale[0], pool_out[0], ret_decay[0], ret_norm_g[0], ret_out[0],
                  w_out[0], peer_wq[0], peer_keys[0], peer_u[0], peer_v[0], final_g)
```

```python
import functools

import jax
import jax.numpy as jnp
import numpy as np
from jax import lax
from jax.experimental import pallas as pl
from jax.experimental.pallas import tpu as pltpu

F32 = jnp.float32
BF16 = jnp.bfloat16
I32 = jnp.int32

D_MODEL = 1024
GRID_W = 64
EPS = 1e-6

POOL_WINDOWS = (2, 4, 8, 16)
POOL_WIDTH = D_MODEL // 2
POOL_GROUP = POOL_WIDTH // len(POOL_WINDOWS)

RET_HEADS = 4
RET_DK = 128
RET_DV = 256
RET_CHUNK = 128
ROPE_BASE = 10000.0
QK_WIDTH = RET_HEADS * RET_DK
V_WIDTH = RET_HEADS * RET_DV
K_SCALE = RET_DK ** -0.5

OFF_POOL = 0
OFF_Q = OFF_POOL + POOL_WIDTH
OFF_K = OFF_Q + QK_WIDTH
OFF_V = OFF_K + QK_WIDTH
OFF_G = OFF_V + V_WIDTH
OFF_MERGE = OFF_G + V_WIDTH
IN_WIDTH = OFF_MERGE + 2 * D_MODEL

PEER_HEADS = 8
PEER_NKEYS = 128
PEER_EXPERTS = PEER_NKEYS * PEER_NKEYS
PEER_TOPK = 16
PEER_DQ = 256
PEER_SLOTS = PEER_HEADS * PEER_TOPK

LANES = 128
SUBLANES = 8
ROW_TILES = D_MODEL // LANES
PACKED_ROWS = ROW_TILES // 2
VMEM_LIMIT = 56 * 1024 * 1024

_NT = (((1,), (1,)), ((), ()))
_TN = (((0,), (0,)), ((), ()))


def _params(sem, vmem=None):
    return pltpu.CompilerParams(dimension_semantics=sem, vmem_limit_bytes=vmem)


def _resident(shape):
    nd = len(shape)
    return pl.BlockSpec(shape, lambda *_: (0,) * nd, pipeline_mode=pl.Buffered(1))


def _rms_mod(xf, g, shift, scale):
    y = xf * lax.rsqrt(jnp.mean(xf * xf, axis=-1, keepdims=True) + EPS)
    return (y * g) * (1.0 + scale) + shift


def _ada_body(c_ref, w_ref, b_ref, o_ref):
    c = c_ref[...]
    s = c * jax.nn.sigmoid(c)
    o_ref[...] = jnp.dot(s, w_ref[...], preferred_element_type=F32,
                         precision=lax.Precision.HIGHEST) + b_ref[...]


def ada_stage(cc, ada_w, ada_b, tn=512):
    r, d = cc.shape
    n = ada_w.shape[1]
    return pl.pallas_call(
        _ada_body,
        grid=(n // tn,),
        in_specs=[pl.BlockSpec((r, d), lambda j: (0, 0)),
                  pl.BlockSpec((d, tn), lambda j: (0, j)),
                  pl.BlockSpec((1, tn), lambda j: (0, j))],
        out_specs=pl.BlockSpec((r, tn), lambda j: (0, j)),
        out_shape=jax.ShapeDtypeStruct((r, n), F32),
        compiler_params=_params(("parallel",)),
        name="ada",
    )(cc, ada_w, ada_b.reshape(1, n))


def _ctx_body(lg_ref, ctx_ref, g_ref, sh_ref, sc_ref, w_ref, sf_ref, sb_ref):
    lc = ctx_ref.shape[0]
    hc = _rms_mod(ctx_ref[...], g_ref[...], sh_ref[...], sc_ref[...])
    kv = jnp.dot(hc.astype(BF16), w_ref[...], preferred_element_type=F32)
    m = lax.broadcasted_iota(I32, (lc, RET_DK), 0).astype(F32)
    for h in range(RET_HEADS):
        kf = kv[:, h * RET_DK:(h + 1) * RET_DK] * K_SCALE
        vb = kv[:, QK_WIDTH + h * RET_DV:QK_WIDTH + (h + 1) * RET_DV].astype(BF16)
        wf = jnp.exp(lg_ref[0, h] * (lc - 1.0 - m))
        wb = jnp.exp(lg_ref[1, h] * m)
        sf_ref[h] = lax.dot_general((kf * wf).astype(BF16), vb, _TN, preferred_element_type=F32)
        sb_ref[h] = lax.dot_general((kf * wb).astype(BF16), vb, _TN, preferred_element_type=F32)


def ctx_stage(lg, ctx, norm_g, csh, csc, w_kv):
    b, lc, d = ctx.shape
    vec = pl.BlockSpec((1, d), lambda i: (0, 0))
    st = jax.ShapeDtypeStruct((b, RET_HEADS, RET_DK, RET_DV), F32)
    st_spec = pl.BlockSpec((None, RET_HEADS, RET_DK, RET_DV), lambda i: (i, 0, 0, 0))
    return pl.pallas_call(
        _ctx_body,
        grid=(b,),
        in_specs=[pl.BlockSpec(memory_space=pltpu.SMEM),
                  pl.BlockSpec((None, lc, d), lambda i: (i, 0, 0)),
                  vec, vec, vec,
                  pl.BlockSpec(w_kv.shape, lambda i: (0, 0))],
        out_specs=[st_spec, st_spec],
        out_shape=[st, st],
        compiler_params=_params(("parallel",)),
        name="ctx",
    )(lg, ctx, norm_g.reshape(1, d), csh.reshape(1, d), csc.reshape(1, d), w_kv)


def _rope(a, cos, sin_signed):
    return a * cos + pltpu.roll(a, RET_DK // 2, 1) * sin_signed


def _proj_body(x_ref, g_ref, sh_ref, sc_ref, cos_ref, sin_ref, w_ref,
               p_ref, q_ref, k_ref, v_ref, gz_ref, mg_ref):
    h = _rms_mod(x_ref[...], g_ref[...], sh_ref[...], sc_ref[...]).astype(BF16)

    def mm(lo, hi):
        return jnp.dot(h, w_ref[:, lo:hi], preferred_element_type=F32)

    p_ref[...] = mm(OFF_POOL, OFF_Q)
    cos = cos_ref[...]
    sin = sin_ref[...]
    qf = mm(OFF_Q, OFF_K)
    kf = mm(OFF_K, OFF_V)
    for hd in range(RET_HEADS):
        sl = slice(hd * RET_DK, (hd + 1) * RET_DK)
        q_ref[:, sl] = _rope(qf[:, sl], cos, sin).astype(BF16)
        k_ref[:, sl] = (_rope(kf[:, sl], cos, sin) * K_SCALE).astype(BF16)
    v_ref[...] = mm(OFF_V, OFF_G).astype(BF16)
    gz_ref[...] = mm(OFF_G, OFF_MERGE).astype(BF16)
    mg_ref[...] = mm(OFF_MERGE, IN_WIDTH).astype(BF16)


def proj_stage(x, norm_g, sh1, sc1, cos, sin, w_in_bf, tm=512):
    b, l, d = x.shape
    vec_b = pl.BlockSpec((None, 1, d), lambda i, j: (i, 0, 0))
    rows = lambda w: pl.BlockSpec((None, tm, w), lambda i, j: (i, j, 0))
    tab = pl.BlockSpec((tm, RET_DK), lambda i, j: (j, 0))
    outs = [(POOL_WIDTH, F32), (QK_WIDTH, BF16), (QK_WIDTH, BF16), (V_WIDTH, BF16),
            (V_WIDTH, BF16), (2 * D_MODEL, BF16)]
    return pl.pallas_call(
        _proj_body,
        grid=(b, l // tm),
        in_specs=[rows(d), pl.BlockSpec((1, d), lambda i, j: (0, 0)), vec_b, vec_b, tab, tab,
                  _resident(w_in_bf.shape)],
        out_specs=[rows(w) for w, _ in outs],
        out_shape=[jax.ShapeDtypeStruct((b, l, w), dt) for w, dt in outs],
        compiler_params=_params(("parallel", "parallel"), VMEM_LIMIT),
        name="proj",
    )(x, norm_g.reshape(1, d), sh1, sc1, cos, sin, w_in_bf)


def _ret_body(lg_ref, q_ref, k_ref, v_ref, sf_ref, sb_ref, y_ref, sfw_scr, sbw_scr, yb_scr):
    c = RET_CHUNK
    n_chunks = q_ref.shape[0] // c
    hd = pl.program_id(1)
    lgf = lg_ref[0, hd]
    lgb = lg_ref[1, hd]
    n_i = lax.broadcasted_iota(I32, (c, c), 0)
    m_i = lax.broadcasted_iota(I32, (c, c), 1)
    rel = (n_i - m_i).astype(F32)
    intra_f = jnp.where(rel >= 0, jnp.exp(lgf * jnp.where(rel >= 0, rel, 0.0)), 0.0)
    intra_b = jnp.where(rel < 0, jnp.exp(lgb * jnp.where(rel < 0, -rel, 0.0)), 0.0)
    pos = lax.broadcasted_iota(I32, (c, RET_DK), 0).astype(F32)
    qdec_f = jnp.exp(lgf * (pos + 1.0))
    kdec_f = jnp.exp(lgf * (c - 1.0 - pos))
    qdec_b = jnp.exp(lgb * (c - pos))
    kdec_b = jnp.exp(lgb * pos)
    blk_f = jnp.exp(jnp.full((1, RET_DV), lgf * c, F32))
    blk_b = jnp.exp(jnp.full((1, RET_DV), lgb * c, F32))

    def chunk(i, st_ref, intra, qdec, kdec, blk):
        rows = pl.ds(pl.multiple_of(i * c, c), c)
        qi = q_ref[rows, :]
        ki = k_ref[rows, :]
        vi = v_ref[rows, :]
        sc = lax.dot_general(qi, ki, _NT, preferred_element_type=F32) * intra
        s = st_ref[...]
        y = (jnp.dot(sc.astype(BF16), vi, preferred_element_type=F32)
             + jnp.dot((qi.astype(F32) * qdec).astype(BF16), s.astype(BF16),
                       preferred_element_type=F32))
        st_ref[...] = s * blk + lax.dot_general((ki.astype(F32) * kdec).astype(BF16), vi, _TN,
                                                preferred_element_type=F32)
        return rows, y

    sfw_scr[...] = sf_ref[...]
    sbw_scr[...] = sb_ref[...]

    def step(i, carry):
        rows, y = chunk(i, sfw_scr, intra_f, qdec_f, kdec_f, blk_f)
        y_ref[rows, :] = y
        rows, y = chunk(n_chunks - 1 - i, sbw_scr, intra_b, qdec_b, kdec_b, blk_b)
        yb_scr[rows, :] = y
        return carry

    lax.fori_loop(0, n_chunks, step, 0)
    y_ref[...] += yb_scr[...]


def ret_stage(lg, q, k, v, s_f, s_b):
    b, l, _ = q.shape
    qk_spec = pl.BlockSpec((None, l, RET_DK), lambda i, j: (i, 0, j))
    v_spec = pl.BlockSpec((None, l, RET_DV), lambda i, j: (i, 0, j))
    st_spec = pl.BlockSpec((None, None, RET_DK, RET_DV), lambda i, j: (i, j, 0, 0))
    return pl.pallas_call(
        _ret_body,
        grid=(b, RET_HEADS),
        in_specs=[pl.BlockSpec(memory_space=pltpu.SMEM), qk_spec, qk_spec, v_spec, st_spec, st_spec],
        out_specs=v_spec,
        out_shape=jax.ShapeDtypeStruct((b, l, V_WIDTH), F32),
        scratch_shapes=[pltpu.VMEM((RET_DK, RET_DV), F32), pltpu.VMEM((RET_DK, RET_DV), F32),
                        pltpu.VMEM((l, RET_DV), F32)],
        compiler_params=_params(("parallel", "parallel")),
        name="ret",
    )(lg, q, k, v, s_f, s_b)


def _pool_bands(tm):
    r = np.arange(tm)[:, None]
    c = np.arange(tm)[None, :]
    bands = np.zeros((len(POOL_WINDOWS), 3, tm, tm), np.float32)
    for gi, w in enumerate(POOL_WINDOWS):
        lo, hi = r - w // 2, r + w - w // 2
        for j, off in enumerate((-tm, 0, tm)):
            bands[gi, j] = ((c + off >= lo) & (c + off < hi)).astype(np.float32)
    return jnp.asarray(bands, BF16)


def _mix_body(pp_ref, pm_ref, pn_ref, y_ref, gz_ref, mg_ref, x_ref, g1_ref, sh2_ref, sc2_ref,
              band_ref, pw_ref, ps_ref, po_ref, rg_ref, ro_ref, wo_ref, ng_ref,
              hf_ref, x1t_ref, hft_ref, *, seq_len):
    tm = pm_ref.shape[0]
    li = pl.program_id(1)
    has_prev = (li > 0).astype(F32)
    has_next = (li < pl.num_programs(1) - 1).astype(F32)
    t = li * tm + lax.broadcasted_iota(I32, (tm, POOL_GROUP), 0)

    def window_sum(ref, cols, gi, j):
        pf = ref[:, cols]
        hi = pf.astype(BF16)
        lo = (pf - hi.astype(F32)).astype(BF16)
        band = band_ref[gi, j]
        return (jnp.dot(band, hi, preferred_element_type=F32)
                + jnp.dot(band, lo, preferred_element_type=F32))

    mixed = []
    for gi, w in enumerate(POOL_WINDOWS):
        cols = slice(gi * POOL_GROUP, (gi + 1) * POOL_GROUP)
        ws = (window_sum(pm_ref, cols, gi, 1) + has_prev * window_sum(pp_ref, cols, gi, 0)
              + has_next * window_sum(pn_ref, cols, gi, 2))
        cnt = (jnp.clip(t + (w - w // 2), 0, seq_len) - jnp.clip(t - w // 2, 0, seq_len)).astype(F32)
        dgi = ws / cnt - pm_ref[:, cols]
        mixed.append(jnp.dot(dgi.astype(BF16), pw_ref[gi], preferred_element_type=F32))
    mixed = jnp.concatenate(mixed, axis=1) * ps_ref[...]
    pool = jnp.dot(mixed.astype(BF16), po_ref[...], preferred_element_type=F32)

    yn = []
    for hd in range(RET_HEADS):
        yh = y_ref[:, hd * RET_DV:(hd + 1) * RET_DV]
        mu = jnp.mean(yh, axis=-1, keepdims=True)
        yc = yh - mu
        var = jnp.mean(yc * yc, axis=-1, keepdims=True)
        yn.append(yc * lax.rsqrt(var + EPS))
    yn = jnp.concatenate(yn, axis=1) * rg_ref[...]
    gate = gz_ref[...].astype(F32)
    ret = jnp.dot((yn * (gate * jax.nn.sigmoid(gate))).astype(BF16), ro_ref[...],
                  preferred_element_type=F32)

    g_pool = mg_ref[:, :D_MODEL].astype(F32)
    g_ret = mg_ref[:, D_MODEL:].astype(F32)
    merged = jax.nn.sigmoid(g_pool) * pool + jax.nn.sigmoid(g_ret) * ret
    out = jnp.dot(merged.astype(BF16), wo_ref[...], preferred_element_type=F32)
    x1 = x_ref[...] + g1_ref[...] * out
    hf = _rms_mod(x1, ng_ref[...], sh2_ref[...], sc2_ref[...])
    hf_ref[...] = hf.astype(BF16)
    for r in range(tm // SUBLANES):
        for c in range(ROW_TILES):
            dst = pl.ds(r * SUBLANES * ROW_TILES + c, SUBLANES, stride=ROW_TILES)
            src = (slice(r * SUBLANES, (r + 1) * SUBLANES), slice(c * LANES, (c + 1) * LANES))
            x1t_ref[dst, :] = x1[src]
            hft_ref[dst, :] = hf[src]


def mix_stage(p, y, gz, mg, x, g1, sh2, sc2, pool_w_bf, pool_scale, pool_out_bf, ret_norm_g,
              ret_out_bf, w_out_bf, norm_ffn_g, tm=256):
    b, l, d = x.shape
    nl = l // tm
    bands = _pool_bands(tm)
    rows = lambda w: pl.BlockSpec((None, tm, w), lambda i, j: (i, j, 0))
    vec_b = pl.BlockSpec((None, 1, d), lambda i, j: (i, 0, 0))
    const = lambda a: pl.BlockSpec(a.shape, lambda i, j: (0,) * a.ndim)
    tiles = pl.BlockSpec((tm * ROW_TILES, LANES), lambda i, j: (i * nl + j, 0))
    ps = pool_scale.reshape(1, POOL_WIDTH)
    rg = ret_norm_g.reshape(1, V_WIDTH)
    ng = norm_ffn_g.reshape(1, d)
    return pl.pallas_call(
        functools.partial(_mix_body, seq_len=l),
        grid=(b, nl),
        in_specs=[pl.BlockSpec((None, tm, POOL_WIDTH), lambda i, j: (i, jnp.maximum(j - 1, 0), 0)),
                  rows(POOL_WIDTH),
                  pl.BlockSpec((None, tm, POOL_WIDTH), lambda i, j: (i, jnp.minimum(j + 1, nl - 1), 0)),
                  rows(V_WIDTH), rows(V_WIDTH), rows(2 * D_MODEL), rows(d), vec_b, vec_b, vec_b,
                  const(bands), const(pool_w_bf), const(ps), const(pool_out_bf), const(rg),
                  const(ret_out_bf), const(w_out_bf), const(ng)],
        out_specs=[rows(d), tiles, tiles],
        out_shape=[jax.ShapeDtypeStruct((b, l, d), BF16),
                   jax.ShapeDtypeStruct((b * l * ROW_TILES, LANES), F32),
                   jax.ShapeDtypeStruct((b * l * ROW_TILES, LANES), F32)],
        compiler_params=_params(("parallel", "parallel"), VMEM_LIMIT),
        name="mix",
    )(p, p, p, y, gz, mg, x, g1, sh2, sc2, bands, pool_w_bf, ps, pool_out_bf, rg, ret_out_bf,
      w_out_bf, ng)


def _topk_rows(s, k):
    r, n = s.shape
    rows = lax.broadcasted_iota(I32, (r, n), 0)
    slot = lax.broadcasted_iota(I32, (k, n), 0)
    vals = jnp.zeros((k, n), F32)
    idxs = jnp.zeros((k, n), I32)
    for j in range(k):
        m = jnp.max(s, axis=0, keepdims=True)
        i = jnp.min(jnp.where(s == m, rows, r), axis=0, keepdims=True)
        vals = jnp.where(slot == j, m, vals)
        idxs = jnp.where(slot == j, i, idxs)
        s = jnp.where(rows == i, -jnp.inf, s)
    return vals, idxs


def _pick_rows(table, sel):
    out = jnp.zeros(sel.shape, table.dtype)
    for r in range(table.shape[0]):
        out = jnp.where(sel == r, table[r:r + 1, :], out)
    return out


_HALF_K = PEER_TOPK // 2
assert PEER_TOPK == 16 and SUBLANES == _HALF_K
_CAND_MID = PEER_TOPK + (_HALF_K - 1) * _HALF_K


def _candidates(s1, s2):
    parts = [s1[0:1, :] + s2]
    parts += [s1[i:i + 1, :] + s2[0:_HALF_K, :] for i in range(1, _HALF_K)]
    parts.append(s1[_HALF_K:, :] + s2[0:1, :])
    return jnp.concatenate(parts, axis=0)


def _candidate_ij(pos):
    mid = pos - PEER_TOPK
    i = jnp.where(pos < PEER_TOPK, 0, jnp.where(pos < _CAND_MID, (mid >> 3) + 1, pos - (_CAND_MID - _HALF_K)))
    j = jnp.where(pos < PEER_TOPK, pos, jnp.where(pos < _CAND_MID, mid & (_HALF_K - 1), 0))
    return i, j


def _route_body(hf_ref, wq_ref, keys_ref, e_ref, g_ref):
    tq = hf_ref.shape[0]
    half = PEER_DQ // 2
    q = jnp.dot(hf_ref[...], wq_ref[...], preferred_element_type=F32).astype(BF16)
    for cb in range(tq // LANES):
        tok = slice(cb * LANES, (cb + 1) * LANES)
        e_heads, g_heads = [], []
        for hd in range(PEER_HEADS):
            sub = []
            for part in range(2):
                col = (hd * 2 + part) * half
                st = lax.dot_general(keys_ref[hd, part], q[tok, col:col + half], _NT,
                                     preferred_element_type=F32)
                sub.append(_topk_rows(st, PEER_TOPK))
            (s1, i1), (s2, i2) = sub
            best, pos = _topk_rows(_candidates(s1, s2), PEER_TOPK)
            ci, cj = _candidate_ij(pos)
            e1 = _pick_rows(i1, ci)
            e2 = _pick_rows(i2, cj)
            ex = jnp.exp(best - best[0:1, :])
            e_heads.append((e1 * PEER_NKEYS + e2) * PACKED_ROWS)
            g_heads.append(ex / jnp.sum(ex, axis=0, keepdims=True))
        e_ref[tok, :] = jnp.concatenate(e_heads, axis=0).T
        g_ref[tok, :] = jnp.concatenate(g_heads, axis=0).T


def route_stage(hf, wq_bf, keys_bf, tq=256):
    n, d = hf.shape
    rows = lambda w: pl.BlockSpec((tq, w), lambda i: (i, 0))
    return pl.pallas_call(
        _route_body,
        grid=(n // tq,),
        in_specs=[rows(d), pl.BlockSpec(wq_bf.shape, lambda i: (0, 0)),
                  pl.BlockSpec(keys_bf.shape, lambda i: (0, 0, 0, 0))],
        out_specs=[rows(PEER_SLOTS), rows(PEER_SLOTS)],
        out_shape=[jax.ShapeDtypeStruct((n, PEER_SLOTS), I32),
                   jax.ShapeDtypeStruct((n, PEER_SLOTS), F32)],
        compiler_params=_params(("parallel",), VMEM_LIMIT),
        name="route",
    )(hf, wq_bf, keys_bf)


def _pack_table(t):
    tb = t.astype(BF16).reshape(PEER_EXPERTS, PACKED_ROWS, 2, LANES).transpose(0, 1, 3, 2)
    return lax.bitcast_convert_type(tb, jnp.uint32).reshape(PEER_EXPERTS * PACKED_ROWS, LANES)


def _table_tile(words):
    return pltpu.bitcast(words, BF16).astype(F32)


OFF_BITS = 16
assert PEER_EXPERTS * PACKED_ROWS <= 1 << OFF_BITS
PEER_TB = 64
_HALF_TB = PEER_TB // 2
_HALF_WORDS = _HALF_TB * PEER_SLOTS


def _offset_copy(off_hbm, bufs, sems, step, h):
    start = (2 * step + h) * _HALF_WORDS
    return pltpu.make_async_copy(off_hbm.at[pl.ds(start, _HALF_WORDS)], bufs[h], sems.at[h])


def _for_each_half(off_hbm, bufs, sems, half_fn):
    step = pl.program_id(0)

    @pl.when(step == 0)
    def _():
        _offset_copy(off_hbm, bufs, sems, step, 0).start()

    _offset_copy(off_hbm, bufs, sems, step, 1).start()
    _offset_copy(off_hbm, bufs, sems, step, 0).wait()
    half_fn(bufs[0], 0)

    @pl.when(step + 1 < pl.num_programs(0))
    def _():
        _offset_copy(off_hbm, bufs, sems, step + 1, 0).start()

    _offset_copy(off_hbm, bufs, sems, step, 1).wait()
    half_fn(bufs[1], _HALF_TB)


def _gather_row(tbl_ref, off):
    return tbl_ref[pl.ds(pl.multiple_of(off, PACKED_ROWS), PACKED_ROWS), :]


_OFFSET_SCRATCH = [pltpu.SMEM((_HALF_WORDS,), I32), pltpu.SMEM((_HALF_WORDS,), I32),
                   pltpu.SemaphoreType.DMA((2,))]


def _peer_u_body(off_hbm, h_ref, g_ref, tbl_ref, o_ref, idx_a, idx_b, sems, s_scr):
    eye = (lax.broadcasted_iota(I32, (PEER_SLOTS, LANES), 0)
           == lax.broadcasted_iota(I32, (PEER_SLOTS, LANES), 1))
    octet = SUBLANES * SUBLANES
    sel = (lax.broadcasted_iota(I32, (SUBLANES, octet), 1) // SUBLANES
           == lax.broadcasted_iota(I32, (SUBLANES, octet), 0)).astype(BF16)

    def half(idx_ref, first):
        for t in range(_HALF_TB):
            tok = first + t
            hv = h_ref[pl.ds(tok * ROW_TILES, ROW_TILES), :]
            for j in range(PEER_SLOTS // SUBLANES):
                prods = [_table_tile(_gather_row(tbl_ref, idx_ref[t * PEER_SLOTS + k])) * hv
                         for k in range(j * SUBLANES, (j + 1) * SUBLANES)]
                stack = jnp.concatenate(prods, axis=0)
                acc = jnp.dot(sel, stack.astype(BF16), preferred_element_type=F32)
                s_scr[pl.ds(tok * PEER_SLOTS + j * SUBLANES, SUBLANES), :] = acc
        for r0 in range(first, first + _HALF_TB, SUBLANES):
            acts = []
            for i in range(SUBLANES):
                tot = jnp.sum(s_scr[pl.ds((r0 + i) * PEER_SLOTS, PEER_SLOTS), :], axis=1, keepdims=True)
                acts.append(jnp.sum(jnp.where(eye, tot, 0.0), axis=0, keepdims=True))
            act = jnp.concatenate(acts, axis=0)
            rows8 = pl.ds(r0, SUBLANES)
            o_ref[rows8, :] = g_ref[rows8, :] * (0.5 * act * (1.0 + lax.erf(act * (2.0 ** -0.5))))

    _for_each_half(off_hbm, (idx_a, idx_b), sems, half)


def _token_tiles():
    return pl.BlockSpec((PEER_TB * ROW_TILES, LANES), lambda i: (i, 0))


def peer_u_stage(eoff, hf_tiles, gates, tbl):
    n = eoff.shape[0]
    rows = lambda: pl.BlockSpec((PEER_TB, PEER_SLOTS), lambda i: (i, 0))
    return pl.pallas_call(
        _peer_u_body,
        grid=(n // PEER_TB,),
        in_specs=[pl.BlockSpec(memory_space=pl.ANY), _token_tiles(), rows(), _resident(tbl.shape)],
        out_specs=rows(),
        out_shape=jax.ShapeDtypeStruct((n, PEER_SLOTS), F32),
        scratch_shapes=_OFFSET_SCRATCH + [pltpu.VMEM((PEER_TB * PEER_SLOTS, LANES), F32)],
        compiler_params=_params(("arbitrary",), VMEM_LIMIT),
        name="peer_u",
    )(eoff.reshape(-1), hf_tiles, gates, tbl)


_V_ACCS = 4


def _peer_v_body(off_hbm, w_ref, x_ref, g2_ref, fg_ref, tbl_ref, o_ref, idx_a, idx_b, sems, acc_scr):
    def half(idx_ref, first):
        for t in range(_HALF_TB):
            tok = first + t
            wrep = jnp.broadcast_to(w_ref[tok], (LANES, PEER_SLOTS)).T
            accs = [None] * _V_ACCS
            for k in range(PEER_SLOTS):
                row = _table_tile(_gather_row(tbl_ref, idx_ref[t * PEER_SLOTS + k]))
                term = jnp.broadcast_to(wrep[k:k + 1, :], (ROW_TILES, LANES)) * row
                accs[k % _V_ACCS] = term if accs[k % _V_ACCS] is None else accs[k % _V_ACCS] + term
            acc_scr[pl.ds(tok * ROW_TILES, ROW_TILES), :] = (accs[0] + accs[1]) + (accs[2] + accs[3])
        rows = pl.ds(first, _HALF_TB)
        x2 = []
        for c in range(ROW_TILES):
            block = pl.ds(first * ROW_TILES + c, _HALF_TB, stride=ROW_TILES)
            x2.append(x_ref[block, :] + g2_ref[c:c + 1, :] * acc_scr[block, :])
        ms = sum(jnp.sum(v * v, axis=1, keepdims=True) for v in x2) / D_MODEL
        inv = lax.rsqrt(ms + EPS)
        for c in range(ROW_TILES):
            o_ref[rows, c * LANES:(c + 1) * LANES] = (x2[c] * inv) * fg_ref[c:c + 1, :]

    _for_each_half(off_hbm, (idx_a, idx_b), sems, half)


def peer_v_stage(eoff, w, x_tiles, g2, final_g, tbl, tokens_per_batch):
    n = eoff.shape[0]
    per_b = tokens_per_batch // PEER_TB
    return pl.pallas_call(
        _peer_v_body,
        grid=(n // PEER_TB,),
        in_specs=[pl.BlockSpec(memory_space=pl.ANY),
                  pl.BlockSpec((PEER_TB, 1, PEER_SLOTS), lambda i: (i, 0, 0)), _token_tiles(),
                  pl.BlockSpec((None, ROW_TILES, LANES), lambda i: (i // per_b, 0, 0)),
                  pl.BlockSpec((ROW_TILES, LANES), lambda i: (0, 0)),
                  _resident(tbl.shape)],
        out_specs=pl.BlockSpec((PEER_TB, D_MODEL), lambda i: (i, 0)),
        out_shape=jax.ShapeDtypeStruct((n, D_MODEL), F32),
        scratch_shapes=_OFFSET_SCRATCH + [pltpu.VMEM((PEER_TB * ROW_TILES, LANES), F32)],
        compiler_params=_params(("arbitrary",), VMEM_LIMIT),
        name="peer_v",
    )(eoff.reshape(-1), w.reshape(n, 1, PEER_SLOTS), x_tiles, g2, final_g.reshape(ROW_TILES, LANES), tbl)


def _rope_tables(l):
    quarter = RET_DK // 4
    rows = l // GRID_W
    row = jnp.repeat(jnp.arange(rows, dtype=F32), GRID_W)
    col = jnp.tile(jnp.arange(GRID_W, dtype=F32), rows)
    inv = ROPE_BASE ** (-jnp.arange(quarter, dtype=F32) / quarter)
    ang = jnp.concatenate([row[:, None] * inv, col[:, None] * inv], axis=-1)
    cos, sin = jnp.cos(ang), jnp.sin(ang)
    return jnp.concatenate([cos, cos], axis=-1), jnp.concatenate([-sin, sin], axis=-1)


def _layer(x, ctx, c, c_ctx, ada_w, ada_b, norm_mix_g, norm_ffn_g, w_in, pool_w, pool_scale,
           pool_out, ret_decay, ret_norm_g, ret_out, w_out, peer_wq, peer_keys, peer_u, peer_v,
           final_g):
    b, l, d = x.shape
    n = b * l

    rows = -(-(b + 1) // SUBLANES) * SUBLANES
    cc = jnp.zeros((rows, d), F32).at[:b].set(c).at[b].set(c_ctx)
    mod = ada_stage(cc, ada_w, ada_b)
    sh1, sc1, g1, sh2, sc2, g2 = [m.reshape(b, 1, d) for m in jnp.split(mod[:b], 6, axis=-1)]
    csh1, csc1 = mod[b, :d], mod[b, d:2 * d]
    lg = jax.nn.log_sigmoid(ret_decay.astype(F32))

    w_in_bf = w_in.astype(BF16)
    s_f, s_b = ctx_stage(lg, ctx, norm_mix_g, csh1, csc1, w_in_bf[:, OFF_K:OFF_G])

    cos, sin = _rope_tables(l)
    p, q, k, v, gz, mg = proj_stage(x, norm_mix_g, sh1, sc1, cos, sin, w_in_bf)
    y = ret_stage(lg, q, k, v, s_f, s_b)
    hf_bf, x1_tiles, hf_tiles = mix_stage(p, y, gz, mg, x, g1, sh2, sc2, pool_w.astype(BF16), pool_scale,
                                          pool_out.astype(BF16), ret_norm_g, ret_out.astype(BF16),
                                          w_out.astype(BF16), norm_ffn_g)

    eoff, gates = route_stage(hf_bf.reshape(n, d), peer_wq.astype(BF16), peer_keys.astype(BF16))
    w = peer_u_stage(eoff, hf_tiles, gates, _pack_table(peer_u))
    out = peer_v_stage(eoff, w, x1_tiles, g2.reshape(b, ROW_TILES, LANES), final_g, _pack_table(peer_v), l)
    return out.reshape(b, l, d)


def kernel(x, c, ctx, c_ctx, ada_w, ada_b, norm_mix_g, norm_ffn_g, w_in, pool_w, pool_scale, pool_out, ret_decay, ret_norm_g, ret_out, w_out, peer_wq, peer_keys, peer_u, peer_v, final_g):
    assert ada_w.shape[0] == 1, "single-layer block"
    return _layer(x, ctx, c, c_ctx, ada_w[0], ada_b[0], norm_mix_g[0], norm_ffn_g[0], w_in[0],
                  pool_w[0], pool_scale[0], pool_out[0], ret_decay[0], ret_norm_g[0], ret_out[0],
                  w_out[0], peer_wq[0], peer_keys[0], peer_u[0], peer_v[0], final_g)
```

```python
import functools

import jax
import jax.numpy as jnp
import numpy as np
from jax import lax
from jax.experimental import pallas as pl
from jax.experimental.pallas import tpu as pltpu

F32 = jnp.float32
BF16 = jnp.bfloat16
I32 = jnp.int32

D_MODEL = 1024
GRID_W = 64
EPS = 1e-6

POOL_WINDOWS = (2, 4, 8, 16)
POOL_WIDTH = D_MODEL // 2
POOL_GROUP = POOL_WIDTH // len(POOL_WINDOWS)

RET_HEADS = 4
RET_DK = 128
RET_DV = 256
RET_CHUNK = 128
ROPE_BASE = 10000.0
QK_WIDTH = RET_HEADS * RET_DK
V_WIDTH = RET_HEADS * RET_DV
K_SCALE = RET_DK ** -0.5

OFF_POOL = 0
OFF_Q = OFF_POOL + POOL_WIDTH
OFF_K = OFF_Q + QK_WIDTH
OFF_V = OFF_K + QK_WIDTH
OFF_G = OFF_V + V_WIDTH
OFF_MERGE = OFF_G + V_WIDTH
IN_WIDTH = OFF_MERGE + 2 * D_MODEL

PEER_HEADS = 8
PEER_NKEYS = 128
PEER_EXPERTS = PEER_NKEYS * PEER_NKEYS
PEER_TOPK = 16
PEER_DQ = 256
PEER_SLOTS = PEER_HEADS * PEER_TOPK

LANES = 128
SUBLANES = 8
ROW_TILES = D_MODEL // LANES
PACKED_ROWS = ROW_TILES // 2
VMEM_LIMIT = 56 * 1024 * 1024

_NT = (((1,), (1,)), ((), ()))
_TN = (((0,), (0,)), ((), ()))


def _params(sem, vmem=None):
    return pltpu.CompilerParams(dimension_semantics=sem, vmem_limit_bytes=vmem)


def _resident(shape):
    nd = len(shape)
    return pl.BlockSpec(shape, lambda *_: (0,) * nd, pipeline_mode=pl.Buffered(1))


def _rms_mod(xf, g, shift, scale):
    y = xf * lax.rsqrt(jnp.mean(xf * xf, axis=-1, keepdims=True) + EPS)
    return (y * g) * (1.0 + scale) + shift


def _ada_body(c_ref, w_ref, b_ref, o_ref):
    c = c_ref[...]
    s = c * jax.nn.sigmoid(c)
    o_ref[...] = jnp.dot(s, w_ref[...], preferred_element_type=F32,
                         precision=lax.Precision.HIGHEST) + b_ref[...]


def ada_stage(cc, ada_w, ada_b, tn=512):
    r, d = cc.shape
    n = ada_w.shape[1]
    return pl.pallas_call(
        _ada_body,
        grid=(n // tn,),
        in_specs=[pl.BlockSpec((r, d), lambda j: (0, 0)),
                  pl.BlockSpec((d, tn), lambda j: (0, j)),
                  pl.BlockSpec((1, tn), lambda j: (0, j))],
        out_specs=pl.BlockSpec((r, tn), lambda j: (0, j)),
        out_shape=jax.ShapeDtypeStruct((r, n), F32),
        compiler_params=_params(("parallel",)),
        name="ada",
    )(cc, ada_w, ada_b.reshape(1, n))


def _ctx_body(lg_ref, ctx_ref, g_ref, sh_ref, sc_ref, w_ref, sf_ref, sb_ref):
    lc = ctx_ref.shape[0]
    hc = _rms_mod(ctx_ref[...], g_ref[...], sh_ref[...], sc_ref[...])
    kv = jnp.dot(hc.astype(BF16), w_ref[...], preferred_element_type=F32)
    m = lax.broadcasted_iota(I32, (lc, RET_DK), 0).astype(F32)
    for h in range(RET_HEADS):
        kf = kv[:, h * RET_DK:(h + 1) * RET_DK] * K_SCALE
        vb = kv[:, QK_WIDTH + h * RET_DV:QK_WIDTH + (h + 1) * RET_DV].astype(BF16)
        wf = jnp.exp(lg_ref[0, h] * (lc - 1.0 - m))
        wb = jnp.exp(lg_ref[1, h] * m)
        sf_ref[h] = lax.dot_general((kf * wf).astype(BF16), vb, _TN, preferred_element_type=F32)
        sb_ref[h] = lax.dot_general((kf * wb).astype(BF16), vb, _TN, preferred_element_type=F32)


def ctx_stage(lg, ctx, norm_g, csh, csc, w_kv):
    b, lc, d = ctx.shape
    vec = pl.BlockSpec((1, d), lambda i: (0, 0))
    st = jax.ShapeDtypeStruct((b, RET_HEADS, RET_DK, RET_DV), F32)
    st_spec = pl.BlockSpec((None, RET_HEADS, RET_DK, RET_DV), lambda i: (i, 0, 0, 0))
    return pl.pallas_call(
        _ctx_body,
        grid=(b,),
        in_specs=[pl.BlockSpec(memory_space=pltpu.SMEM),
                  pl.BlockSpec((None, lc, d), lambda i: (i, 0, 0)),
                  vec, vec, vec,
                  pl.BlockSpec(w_kv.shape, lambda i: (0, 0))],
        out_specs=[st_spec, st_spec],
        out_shape=[st, st],
        compiler_params=_params(("parallel",)),
        name="ctx",
    )(lg, ctx, norm_g.reshape(1, d), csh.reshape(1, d), csc.reshape(1, d), w_kv)


def _rope(a, cos, sin_signed):
    return a * cos + pltpu.roll(a, RET_DK // 2, 1) * sin_signed


def _proj_body(x_ref, g_ref, sh_ref, sc_ref, cos_ref, sin_ref, w_ref,
               p_ref, q_ref, k_ref, v_ref, gz_ref, mg_ref):
    h = _rms_mod(x_ref[...], g_ref[...], sh_ref[...], sc_ref[...]).astype(BF16)

    def mm(lo, hi):
        return jnp.dot(h, w_ref[:, lo:hi], preferred_element_type=F32)

    p_ref[...] = mm(OFF_POOL, OFF_Q)
    cos = cos_ref[...]
    sin = sin_ref[...]
    qf = mm(OFF_Q, OFF_K)
    kf = mm(OFF_K, OFF_V)
    for hd in range(RET_HEADS):
        sl = slice(hd * RET_DK, (hd + 1) * RET_DK)
        q_ref[:, sl] = _rope(qf[:, sl], cos, sin).astype(BF16)
        k_ref[:, sl] = (_rope(kf[:, sl], cos, sin) * K_SCALE).astype(BF16)
    v_ref[...] = mm(OFF_V, OFF_G).astype(BF16)
    gz_ref[...] = mm(OFF_G, OFF_MERGE).astype(BF16)
    mg_ref[...] = mm(OFF_MERGE, IN_WIDTH).astype(BF16)


def proj_stage(x, norm_g, sh1, sc1, cos, sin, w_in_bf, tm=512):
    b, l, d = x.shape
    vec_b = pl.BlockSpec((None, 1, d), lambda i, j: (i, 0, 0))
    rows = lambda w: pl.BlockSpec((None, tm, w), lambda i, j: (i, j, 0))
    tab = pl.BlockSpec((tm, RET_DK), lambda i, j: (j, 0))
    outs = [(POOL_WIDTH, F32), (QK_WIDTH, BF16), (QK_WIDTH, BF16), (V_WIDTH, BF16),
            (V_WIDTH, BF16), (2 * D_MODEL, BF16)]
    return pl.pallas_call(
        _proj_body,
        grid=(b, l // tm),
        in_specs=[rows(d), pl.BlockSpec((1, d), lambda i, j: (0, 0)), vec_b, vec_b, tab, tab,
                  _resident(w_in_bf.shape)],
        out_specs=[rows(w) for w, _ in outs],
        out_shape=[jax.ShapeDtypeStruct((b, l, w), dt) for w, dt in outs],
        compiler_params=_params(("parallel", "parallel"), VMEM_LIMIT),
        name="proj",
    )(x, norm_g.reshape(1, d), sh1, sc1, cos, sin, w_in_bf)


def _ret_body(lg_ref, q_ref, k_ref, v_ref, sf_ref, sb_ref, y_ref, sfw_scr, sbw_scr, yb_scr):
    c = RET_CHUNK
    n_chunks = q_ref.shape[0] // c
    hd = pl.program_id(1)
    lgf = lg_ref[0, hd]
    lgb = lg_ref[1, hd]
    n_i = lax.broadcasted_iota(I32, (c, c), 0)
    m_i = lax.broadcasted_iota(I32, (c, c), 1)
    rel = (n_i - m_i).astype(F32)
    intra_f = jnp.where(rel >= 0, jnp.exp(lgf * jnp.where(rel >= 0, rel, 0.0)), 0.0)
    intra_b = jnp.where(rel < 0, jnp.exp(lgb * jnp.where(rel < 0, -rel, 0.0)), 0.0)
    pos = lax.broadcasted_iota(I32, (c, RET_DK), 0).astype(F32)
    qdec_f = jnp.exp(lgf * (pos + 1.0))
    kdec_f = jnp.exp(lgf * (c - 1.0 - pos))
    qdec_b = jnp.exp(lgb * (c - pos))
    kdec_b = jnp.exp(lgb * pos)
    blk_f = jnp.exp(jnp.full((1, RET_DV), lgf * c, F32))
    blk_b = jnp.exp(jnp.full((1, RET_DV), lgb * c, F32))

    def chunk(i, st_ref, intra, qdec, kdec, blk):
        rows = pl.ds(pl.multiple_of(i * c, c), c)
        qi = q_ref[rows, :]
        ki = k_ref[rows, :]
        vi = v_ref[rows, :]
        sc = lax.dot_general(qi, ki, _NT, preferred_element_type=F32) * intra
        s = st_ref[...]
        y = (jnp.dot(sc.astype(BF16), vi, preferred_element_type=F32)
             + jnp.dot((qi.astype(F32) * qdec).astype(BF16), s.astype(BF16),
                       preferred_element_type=F32))
        st_ref[...] = s * blk + lax.dot_general((ki.astype(F32) * kdec).astype(BF16), vi, _TN,
                                                preferred_element_type=F32)
        return rows, y

    sfw_scr[...] = sf_ref[...]
    sbw_scr[...] = sb_ref[...]

    def step(i, carry):
        rows, y = chunk(i, sfw_scr, intra_f, qdec_f, kdec_f, blk_f)
        y_ref[rows, :] = y
        rows, y = chunk(n_chunks - 1 - i, sbw_scr, intra_b, qdec_b, kdec_b, blk_b)
        yb_scr[rows, :] = y
        return carry

    lax.fori_loop(0, n_chunks, step, 0)
    y_ref[...] += yb_scr[...]


def ret_stage(lg, q, k, v, s_f, s_b):
    b, l, _ = q.shape
    qk_spec = pl.BlockSpec((None, l, RET_DK), lambda i, j: (i, 0, j))
    v_spec = pl.BlockSpec((None, l, RET_DV), lambda i, j: (i, 0, j))
    st_spec = pl.BlockSpec((None, None, RET_DK, RET_DV), lambda i, j: (i, j, 0, 0))
    return pl.pallas_call(
        _ret_body,
        grid=(b, RET_HEADS),
        in_specs=[pl.BlockSpec(memory_space=pltpu.SMEM), qk_spec, qk_spec, v_spec, st_spec, st_spec],
        out_specs=v_spec,
        out_shape=jax.ShapeDtypeStruct((b, l, V_WIDTH), F32),
        scratch_shapes=[pltpu.VMEM((RET_DK, RET_DV), F32), pltpu.VMEM((RET_DK, RET_DV), F32),
                        pltpu.VMEM((l, RET_DV), F32)],
        compiler_params=_params(("parallel", "parallel")),
        name="ret",
    )(lg, q, k, v, s_f, s_b)


def _pool_bands(tm):
    r = np.arange(tm)[:, None]
    c = np.arange(tm)[None, :]
    bands = np.zeros((len(POOL_WINDOWS), 3, tm, tm), np.float32)
    for gi, w in enumerate(POOL_WINDOWS):
        lo, hi = r - w // 2, r + w - w // 2
        for j, off in enumerate((-tm, 0, tm)):
            bands[gi, j] = ((c + off >= lo) & (c + off < hi)).astype(np.float32)
    return jnp.asarray(bands, BF16)


def _mix_body(pp_ref, pm_ref, pn_ref, y_ref, gz_ref, mg_ref, x_ref, g1_ref, sh2_ref, sc2_ref,
              band_ref, pw_ref, ps_ref, po_ref, rg_ref, ro_ref, wo_ref, ng_ref,
              hf_ref, x1t_ref, hft_ref, *, seq_len):
    tm = pm_ref.shape[0]
    li = pl.program_id(1)
    has_prev = (li > 0).astype(F32)
    has_next = (li < pl.num_programs(1) - 1).astype(F32)
    t = li * tm + lax.broadcasted_iota(I32, (tm, POOL_GROUP), 0)

    def window_sum(ref, cols, gi, j):
        pf = ref[:, cols]
        hi = pf.astype(BF16)
        lo = (pf - hi.astype(F32)).astype(BF16)
        band = band_ref[gi, j]
        return (jnp.dot(band, hi, preferred_element_type=F32)
                + jnp.dot(band, lo, preferred_element_type=F32))

    mixed = []
    for gi, w in enumerate(POOL_WINDOWS):
        cols = slice(gi * POOL_GROUP, (gi + 1) * POOL_GROUP)
        ws = (window_sum(pm_ref, cols, gi, 1) + has_prev * window_sum(pp_ref, cols, gi, 0)
              + has_next * window_sum(pn_ref, cols, gi, 2))
        cnt = (jnp.clip(t + (w - w // 2), 0, seq_len) - jnp.clip(t - w // 2, 0, seq_len)).astype(F32)
        dgi = ws / cnt - pm_ref[:, cols]
        mixed.append(jnp.dot(dgi.astype(BF16), pw_ref[gi], preferred_element_type=F32))
    mixed = jnp.concatenate(mixed, axis=1) * ps_ref[...]
    pool = jnp.dot(mixed.astype(BF16), po_ref[...], preferred_element_type=F32)

    yn = []
    for hd in range(RET_HEADS):
        yh = y_ref[:, hd * RET_DV:(hd + 1) * RET_DV]
        mu = jnp.mean(yh, axis=-1, keepdims=True)
        yc = yh - mu
        var = jnp.mean(yc * yc, axis=-1, keepdims=True)
        yn.append(yc * lax.rsqrt(var + EPS))
    yn = jnp.concatenate(yn, axis=1) * rg_ref[...]
    gate = gz_ref[...].astype(F32)
    ret = jnp.dot((yn * (gate * jax.nn.sigmoid(gate))).astype(BF16), ro_ref[...],
                  preferred_element_type=F32)

    g_pool = mg_ref[:, :D_MODEL].astype(F32)
    g_ret = mg_ref[:, D_MODEL:].astype(F32)
    merged = jax.nn.sigmoid(g_pool) * pool + jax.nn.sigmoid(g_ret) * ret
    out = jnp.dot(merged.astype(BF16), wo_ref[...], preferred_element_type=F32)
    x1 = x_ref[...] + g1_ref[...] * out
    hf = _rms_mod(x1, ng_ref[...], sh2_ref[...], sc2_ref[...])
    hf_ref[...] = hf.astype(BF16)
    for r in range(tm // SUBLANES):
        for c in range(ROW_TILES):
            dst = pl.ds(r * SUBLANES * ROW_TILES + c, SUBLANES, stride=ROW_TILES)
            src = (slice(r * SUBLANES, (r + 1) * SUBLANES), slice(c * LANES, (c + 1) * LANES))
            x1t_ref[dst, :] = x1[src]
            hft_ref[dst, :] = hf[src]


def mix_stage(p, y, gz, mg, x, g1, sh2, sc2, pool_w_bf, pool_scale, pool_out_bf, ret_norm_g,
              ret_out_bf, w_out_bf, norm_ffn_g, tm=256):
    b, l, d = x.shape
    nl = l // tm
    bands = _pool_bands(tm)
    rows = lambda w: pl.BlockSpec((None, tm, w), lambda i, j: (i, j, 0))
    vec_b = pl.BlockSpec((None, 1, d), lambda i, j: (i, 0, 0))
    const = lambda a: pl.BlockSpec(a.shape, lambda i, j: (0,) * a.ndim)
    tiles = pl.BlockSpec((tm * ROW_TILES, LANES), lambda i, j: (i * nl + j, 0))
    ps = pool_scale.reshape(1, POOL_WIDTH)
    rg = ret_norm_g.reshape(1, V_WIDTH)
    ng = norm_ffn_g.reshape(1, d)
    return pl.pallas_call(
        functools.partial(_mix_body, seq_len=l),
        grid=(b, nl),
        in_specs=[pl.BlockSpec((None, tm, POOL_WIDTH), lambda i, j: (i, jnp.maximum(j - 1, 0), 0)),
                  rows(POOL_WIDTH),
                  pl.BlockSpec((None, tm, POOL_WIDTH), lambda i, j: (i, jnp.minimum(j + 1, nl - 1), 0)),
                  rows(V_WIDTH), rows(V_WIDTH), rows(2 * D_MODEL), rows(d), vec_b, vec_b, vec_b,
                  const(bands), const(pool_w_bf), const(ps), const(pool_out_bf), const(rg),
                  const(ret_out_bf), const(w_out_bf), const(ng)],
        out_specs=[rows(d), tiles, tiles],
        out_shape=[jax.ShapeDtypeStruct((b, l, d), BF16),
                   jax.ShapeDtypeStruct((b * l * ROW_TILES, LANES), F32),
                   jax.ShapeDtypeStruct((b * l * ROW_TILES, LANES), F32)],
        compiler_params=_params(("parallel", "parallel"), VMEM_LIMIT),
        name="mix",
    )(p, p, p, y, gz, mg, x, g1, sh2, sc2, bands, pool_w_bf, ps, pool_out_bf, rg, ret_out_bf,
      w_out_bf, ng)


def _topk_rows(s, k):
    r, n = s.shape
    rows = lax.broadcasted_iota(I32, (r, n), 0)
    slot = lax.broadcasted_iota(I32, (k, n), 0)
    vals = jnp.zeros((k, n), F32)
    idxs = jnp.zeros((k, n), I32)
    for j in range(k):
        m = jnp.max(s, axis=0, keepdims=True)
        i = jnp.min(jnp.where(s == m, rows, r), axis=0, keepdims=True)
        vals = jnp.where(slot == j, m, vals)
        idxs = jnp.where(slot == j, i, idxs)
        s = jnp.where(rows == i, -jnp.inf, s)
    return vals, idxs


def _pick_rows(table, sel):
    out = jnp.zeros(sel.shape, table.dtype)
    for r in range(table.shape[0]):
        out = jnp.where(sel == r, table[r:r + 1, :], out)
    return out


_HALF_K = PEER_TOPK // 2
assert PEER_TOPK == 16 and SUBLANES == _HALF_K
_CAND_MID = PEER_TOPK + (_HALF_K - 1) * _HALF_K


def _candidates(s1, s2):
    parts = [s1[0:1, :] + s2]
    parts += [s1[i:i + 1, :] + s2[0:_HALF_K, :] for i in range(1, _HALF_K)]
    parts.append(s1[_HALF_K:, :] + s2[0:1, :])
    return jnp.concatenate(parts, axis=0)


def _candidate_ij(pos):
    mid = pos - PEER_TOPK
    i = jnp.where(pos < PEER_TOPK, 0, jnp.where(pos < _CAND_MID, (mid >> 3) + 1, pos - (_CAND_MID - _HALF_K)))
    j = jnp.where(pos < PEER_TOPK, pos, jnp.where(pos < _CAND_MID, mid & (_HALF_K - 1), 0))
    return i, j


def _route_body(hf_ref, wq_ref, keys_ref, e_ref, g_ref):
    tq = hf_ref.shape[0]
    half = PEER_DQ // 2
    q = jnp.dot(hf_ref[...], wq_ref[...], preferred_element_type=F32).astype(BF16)
    for cb in range(tq // LANES):
        tok = slice(cb * LANES, (cb + 1) * LANES)
        e_heads, g_heads = [], []
        for hd in range(PEER_HEADS):
            sub = []
            for part in range(2):
                col = (hd * 2 + part) * half
                st = lax.dot_general(keys_ref[hd, part], q[tok, col:col + half], _NT,
                                     preferred_element_type=F32)
                sub.append(_topk_rows(st, PEER_TOPK))
            (s1, i1), (s2, i2) = sub
            best, pos = _topk_rows(_candidates(s1, s2), PEER_TOPK)
            ci, cj = _candidate_ij(pos)
            e1 = _pick_rows(i1, ci)
            e2 = _pick_rows(i2, cj)
            ex = jnp.exp(best - best[0:1, :])
            e_heads.append((e1 * PEER_NKEYS + e2) * PACKED_ROWS)
            g_heads.append(ex / jnp.sum(ex, axis=0, keepdims=True))
        e_ref[tok, :] = jnp.concatenate(e_heads, axis=0).T
        g_ref[tok, :] = jnp.concatenate(g_heads, axis=0).T


def route_stage(hf, wq_bf, keys_bf, tq=256):
    n, d = hf.shape
    rows = lambda w: pl.BlockSpec((tq, w), lambda i: (i, 0))
    return pl.pallas_call(
        _route_body,
        grid=(n // tq,),
        in_specs=[rows(d), pl.BlockSpec(wq_bf.shape, lambda i: (0, 0)),
                  pl.BlockSpec(keys_bf.shape, lambda i: (0, 0, 0, 0))],
        out_specs=[rows(PEER_SLOTS), rows(PEER_SLOTS)],
        out_shape=[jax.ShapeDtypeStruct((n, PEER_SLOTS), I32),
                   jax.ShapeDtypeStruct((n, PEER_SLOTS), F32)],
        compiler_params=_params(("parallel",), VMEM_LIMIT),
        name="route",
    )(hf, wq_bf, keys_bf)


def _pack_table(t):
    tb = t.astype(BF16).reshape(PEER_EXPERTS, PACKED_ROWS, 2, LANES).transpose(0, 1, 3, 2)
    return lax.bitcast_convert_type(tb, jnp.uint32).reshape(PEER_EXPERTS * PACKED_ROWS, LANES)


def _table_tile(words):
    return pltpu.bitcast(words, BF16).astype(F32)


OFF_BITS = 16
assert PEER_EXPERTS * PACKED_ROWS <= 1 << OFF_BITS
PEER_TB = 128
_HALF_TB = PEER_TB // 2
_HALF_WORDS = _HALF_TB * PEER_SLOTS


def _offset_copy(off_hbm, bufs, sems, step, h):
    start = (2 * step + h) * _HALF_WORDS
    return pltpu.make_async_copy(off_hbm.at[pl.ds(start, _HALF_WORDS)], bufs[h], sems.at[h])


def _for_each_half(off_hbm, bufs, sems, half_fn):
    step = pl.program_id(0)

    @pl.when(step == 0)
    def _():
        _offset_copy(off_hbm, bufs, sems, step, 0).start()

    _offset_copy(off_hbm, bufs, sems, step, 1).start()
    _offset_copy(off_hbm, bufs, sems, step, 0).wait()
    half_fn(bufs[0], 0)

    @pl.when(step + 1 < pl.num_programs(0))
    def _():
        _offset_copy(off_hbm, bufs, sems, step + 1, 0).start()

    _offset_copy(off_hbm, bufs, sems, step, 1).wait()
    half_fn(bufs[1], _HALF_TB)


def _gather_row(tbl_ref, off):
    return tbl_ref[pl.ds(pl.multiple_of(off, PACKED_ROWS), PACKED_ROWS), :]


_OFFSET_SCRATCH = [pltpu.SMEM((_HALF_WORDS,), I32), pltpu.SMEM((_HALF_WORDS,), I32),
                   pltpu.SemaphoreType.DMA((2,))]


def _peer_u_body(off_hbm, h_ref, g_ref, tbl_ref, o_ref, idx_a, idx_b, sems, s_scr):
    eye = (lax.broadcasted_iota(I32, (PEER_SLOTS, LANES), 0)
           == lax.broadcasted_iota(I32, (PEER_SLOTS, LANES), 1))
    octet = SUBLANES * SUBLANES
    sel = (lax.broadcasted_iota(I32, (SUBLANES, octet), 1) // SUBLANES
           == lax.broadcasted_iota(I32, (SUBLANES, octet), 0)).astype(BF16)

    def half(idx_ref, first):
        for t in range(_HALF_TB):
            tok = first + t
            hv = h_ref[pl.ds(tok * ROW_TILES, ROW_TILES), :]
            for j in range(PEER_SLOTS // SUBLANES):
                prods = [_table_tile(_gather_row(tbl_ref, idx_ref[t * PEER_SLOTS + k])) * hv
                         for k in range(j * SUBLANES, (j + 1) * SUBLANES)]
                stack = jnp.concatenate(prods, axis=0)
                acc = jnp.dot(sel, stack.astype(BF16), preferred_element_type=F32)
                s_scr[pl.ds(tok * PEER_SLOTS + j * SUBLANES, SUBLANES), :] = acc
        for r0 in range(first, first + _HALF_TB, SUBLANES):
            acts = []
            for i in range(SUBLANES):
                tot = jnp.sum(s_scr[pl.ds((r0 + i) * PEER_SLOTS, PEER_SLOTS), :], axis=1, keepdims=True)
                acts.append(jnp.sum(jnp.where(eye, tot, 0.0), axis=0, keepdims=True))
            act = jnp.concatenate(acts, axis=0)
            rows8 = pl.ds(r0, SUBLANES)
            o_ref[rows8, :] = g_ref[rows8, :] * (0.5 * act * (1.0 + lax.erf(act * (2.0 ** -0.5))))

    _for_each_half(off_hbm, (idx_a, idx_b), sems, half)


def _token_tiles():
    return pl.BlockSpec((PEER_TB * ROW_TILES, LANES), lambda i: (i, 0))


def peer_u_stage(eoff, hf_tiles, gates, tbl):
    n = eoff.shape[0]
    rows = lambda: pl.BlockSpec((PEER_TB, PEER_SLOTS), lambda i: (i, 0))
    return pl.pallas_call(
        _peer_u_body,
        grid=(n // PEER_TB,),
        in_specs=[pl.BlockSpec(memory_space=pl.ANY), _token_tiles(), rows(), _resident(tbl.shape)],
        out_specs=rows(),
        out_shape=jax.ShapeDtypeStruct((n, PEER_SLOTS), F32),
        scratch_shapes=_OFFSET_SCRATCH + [pltpu.VMEM((PEER_TB * PEER_SLOTS, LANES), F32)],
        compiler_params=_params(("arbitrary",), VMEM_LIMIT),
        name="peer_u",
    )(eoff.reshape(-1), hf_tiles, gates, tbl)


_V_ACCS = 4


def _peer_v_body(off_hbm, w_ref, x_ref, g2_ref, fg_ref, tbl_ref, o_ref, idx_a, idx_b, sems, acc_scr):
    def half(idx_ref, first):
        for t in range(_HALF_TB):
            tok = first + t
            wrep = jnp.broadcast_to(w_ref[tok], (LANES, PEER_SLOTS)).T
            accs = [None] * _V_ACCS
            for k in range(PEER_SLOTS):
                row = _table_tile(_gather_row(tbl_ref, idx_ref[t * PEER_SLOTS + k]))
                term = jnp.broadcast_to(wrep[k:k + 1, :], (ROW_TILES, LANES)) * row
                accs[k % _V_ACCS] = term if accs[k % _V_ACCS] is None else accs[k % _V_ACCS] + term
            acc_scr[pl.ds(tok * ROW_TILES, ROW_TILES), :] = (accs[0] + accs[1]) + (accs[2] + accs[3])
        rows = pl.ds(first, _HALF_TB)
        x2 = []
        for c in range(ROW_TILES):
            block = pl.ds(first * ROW_TILES + c, _HALF_TB, stride=ROW_TILES)
            x2.append(x_ref[block, :] + g2_ref[c:c + 1, :] * acc_scr[block, :])
        ms = sum(jnp.sum(v * v, axis=1, keepdims=True) for v in x2) / D_MODEL
        inv = lax.rsqrt(ms + EPS)
        for c in range(ROW_TILES):
            o_ref[rows, c * LANES:(c + 1) * LANES] = (x2[c] * inv) * fg_ref[c:c + 1, :]

    _for_each_half(off_hbm, (idx_a, idx_b), sems, half)


def peer_v_stage(eoff, w, x_tiles, g2, final_g, tbl, tokens_per_batch):
    n = eoff.shape[0]
    per_b = tokens_per_batch // PEER_TB
    return pl.pallas_call(
        _peer_v_body,
        grid=(n // PEER_TB,),
        in_specs=[pl.BlockSpec(memory_space=pl.ANY),
                  pl.BlockSpec((PEER_TB, 1, PEER_SLOTS), lambda i: (i, 0, 0)), _token_tiles(),
                  pl.BlockSpec((None, ROW_TILES, LANES), lambda i: (i // per_b, 0, 0)),
                  pl.BlockSpec((ROW_TILES, LANES), lambda i: (0, 0)),
                  _resident(tbl.shape)],
        out_specs=pl.BlockSpec((PEER_TB, D_MODEL), lambda i: (i, 0)),
        out_shape=jax.ShapeDtypeStruct((n, D_MODEL), F32),
        scratch_shapes=_OFFSET_SCRATCH + [pltpu.VMEM((PEER_TB * ROW_TILES, LANES), F32)],
        compiler_params=_params(("arbitrary",), VMEM_LIMIT),
        name="peer_v",
    )(eoff.reshape(-1), w.reshape(n, 1, PEER_SLOTS), x_tiles, g2, final_g.reshape(ROW_TILES, LANES), tbl)


def _rope_tables(l):
    quarter = RET_DK // 4
    rows = l // GRID_W
    row = jnp.repeat(jnp.arange(rows, dtype=F32), GRID_W)
    col = jnp.tile(jnp.arange(GRID_W, dtype=F32), rows)
    inv = ROPE_BASE ** (-jnp.arange(quarter, dtype=F32) / quarter)
    ang = jnp.concatenate([row[:, None] * inv, col[:, None] * inv], axis=-1)
    cos, sin = jnp.cos(ang), jnp.sin(ang)
    return jnp.concatenate([cos, cos], axis=-1), jnp.concatenate([-sin, sin], axis=-1)


def _layer(x, ctx, c, c_ctx, ada_w, ada_b, norm_mix_g, norm_ffn_g, w_in, pool_w, pool_scale,
           pool_out, ret_decay, ret_norm_g, ret_out, w_out, peer_wq, peer_keys, peer_u, peer_v,
           final_g):
    b, l, d = x.shape
    n = b * l

    rows = -(-(b + 1) // SUBLANES) * SUBLANES
    cc = jnp.zeros((rows, d), F32).at[:b].set(c).at[b].set(c_ctx)
    mod = ada_stage(cc, ada_w, ada_b)
    sh1, sc1, g1, sh2, sc2, g2 = [m.reshape(b, 1, d) for m in jnp.split(mod[:b], 6, axis=-1)]
    csh1, csc1 = mod[b, :d], mod[b, d:2 * d]
    lg = jax.nn.log_sigmoid(ret_decay.astype(F32))

    w_in_bf = w_in.astype(BF16)
    s_f, s_b = ctx_stage(lg, ctx, norm_mix_g, csh1, csc1, w_in_bf[:, OFF_K:OFF_G])

    cos, sin = _rope_tables(l)
    p, q, k, v, gz, mg = proj_stage(x, norm_mix_g, sh1, sc1, cos, sin, w_in_bf)
    y = ret_stage(lg, q, k, v, s_f, s_b)
    hf_bf, x1_tiles, hf_tiles = mix_stage(p, y, gz, mg, x, g1, sh2, sc2, pool_w.astype(BF16), pool_scale,
                                          pool_out.astype(BF16), ret_norm_g, ret_out.astype(BF16),
                                          w_out.astype(BF16), norm_ffn_g)

    eoff, gates = route_stage(hf_bf.reshape(n, d), peer_wq.astype(BF16), peer_keys.astype(BF16))
    w = peer_u_stage(eoff, hf_tiles, gates, _pack_table(peer_u))
    out = peer_v_stage(eoff, w, x1_tiles, g2.reshape(b, ROW_TILES, LANES), final_g, _pack_table(peer_v), l)
    return out.reshape(b, l, d)


def kernel(x, c, ctx, c_ctx, ada_w, ada_b, norm_mix_g, norm_ffn_g, w_in, pool_w, pool_scale, pool_out, ret_decay, ret_norm_g, ret_out, w_out, peer_wq, peer_keys, peer_u, peer_v, final_g):
    assert ada_w.shape[0] == 1, "single-layer block"
    return _layer(x, ctx, c, c_ctx, ada_w[0], ada_b[0], norm_mix_g[0], norm_ffn_g[0], w_in[0],
                  pool_w[0], pool_scale[0], pool_out[0], ret_decay[0], ret_norm_g[0], ret_out[0],
                  w_out[0], peer_wq[0], peer_keys[0], peer_u[0], peer_v[0], final_g)
```

```python
import functools

import jax
import jax.numpy as jnp
import numpy as np
from jax import lax
from jax.experimental import pallas as pl
from jax.experimental.pallas import tpu as pltpu

F32 = jnp.float32
BF16 = jnp.bfloat16
I32 = jnp.int32

D_MODEL = 1024
GRID_W = 64
EPS = 1e-6

POOL_WINDOWS = (2, 4, 8, 16)
POOL_WIDTH = D_MODEL // 2
POOL_GROUP = POOL_WIDTH // len(POOL_WINDOWS)

RET_HEADS = 4
RET_DK = 128
RET_DV = 256
RET_CHUNK = 128
ROPE_BASE = 10000.0
QK_WIDTH = RET_HEADS * RET_DK
V_WIDTH = RET_HEADS * RET_DV
K_SCALE = RET_DK ** -0.5

OFF_POOL = 0
OFF_Q = OFF_POOL + POOL_WIDTH
OFF_K = OFF_Q + QK_WIDTH
OFF_V = OFF_K + QK_WIDTH
OFF_G = OFF_V + V_WIDTH
OFF_MERGE = OFF_G + V_WIDTH
IN_WIDTH = OFF_MERGE + 2 * D_MODEL

PEER_HEADS = 8
PEER_NKEYS = 128
PEER_EXPERTS = PEER_NKEYS * PEER_NKEYS
PEER_TOPK = 16
PEER_DQ = 256
PEER_SLOTS = PEER_HEADS * PEER_TOPK

LANES = 128
SUBLANES = 8
ROW_TILES = D_MODEL // LANES
PACKED_ROWS = ROW_TILES // 2
VMEM_LIMIT = 56 * 1024 * 1024

_NT = (((1,), (1,)), ((), ()))
_TN = (((0,), (0,)), ((), ()))


def _params(sem, vmem=None):
    return pltpu.CompilerParams(dimension_semantics=sem, vmem_limit_bytes=vmem)


def _resident(shape):
    nd = len(shape)
    return pl.BlockSpec(shape, lambda *_: (0,) * nd, pipeline_mode=pl.Buffered(1))


def _rms_mod(xf, g, shift, scale):
    y = xf * lax.rsqrt(jnp.mean(xf * xf, axis=-1, keepdims=True) + EPS)
    return (y * g) * (1.0 + scale) + shift


def _ada_body(c_ref, w_ref, b_ref, o_ref):
    c = c_ref[...]
    s = c * jax.nn.sigmoid(c)
    o_ref[...] = jnp.dot(s, w_ref[...], preferred_element_type=F32,
                         precision=lax.Precision.HIGHEST) + b_ref[...]


def ada_stage(cc, ada_w, ada_b, tn=512):
    r, d = cc.shape
    n = ada_w.shape[1]
    return pl.pallas_call(
        _ada_body,
        grid=(n // tn,),
        in_specs=[pl.BlockSpec((r, d), lambda j: (0, 0)),
                  pl.BlockSpec((d, tn), lambda j: (0, j)),
                  pl.BlockSpec((1, tn), lambda j: (0, j))],
        out_specs=pl.BlockSpec((r, tn), lambda j: (0, j)),
        out_shape=jax.ShapeDtypeStruct((r, n), F32),
        compiler_params=_params(("parallel",)),
        name="ada",
    )(cc, ada_w, ada_b.reshape(1, n))


def _ctx_body(lg_ref, ctx_ref, g_ref, sh_ref, sc_ref, w_ref, sf_ref, sb_ref):
    lc = ctx_ref.shape[0]
    hc = _rms_mod(ctx_ref[...], g_ref[...], sh_ref[...], sc_ref[...])
    kv = jnp.dot(hc.astype(BF16), w_ref[...], preferred_element_type=F32)
    m = lax.broadcasted_iota(I32, (lc, RET_DK), 0).astype(F32)
    for h in range(RET_HEADS):
        kf = kv[:, h * RET_DK:(h + 1) * RET_DK] * K_SCALE
        vb = kv[:, QK_WIDTH + h * RET_DV:QK_WIDTH + (h + 1) * RET_DV].astype(BF16)
        wf = jnp.exp(lg_ref[0, h] * (lc - 1.0 - m))
        wb = jnp.exp(lg_ref[1, h] * m)
        sf_ref[h] = lax.dot_general((kf * wf).astype(BF16), vb, _TN, preferred_element_type=F32)
        sb_ref[h] = lax.dot_general((kf * wb).astype(BF16), vb, _TN, preferred_element_type=F32)


def ctx_stage(lg, ctx, norm_g, csh, csc, w_kv):
    b, lc, d = ctx.shape
    vec = pl.BlockSpec((1, d), lambda i: (0, 0))
    st = jax.ShapeDtypeStruct((b, RET_HEADS, RET_DK, RET_DV), F32)
    st_spec = pl.BlockSpec((None, RET_HEADS, RET_DK, RET_DV), lambda i: (i, 0, 0, 0))
    return pl.pallas_call(
        _ctx_body,
        grid=(b,),
        in_specs=[pl.BlockSpec(memory_space=pltpu.SMEM),
                  pl.BlockSpec((None, lc, d), lambda i: (i, 0, 0)),
                  vec, vec, vec,
                  pl.BlockSpec(w_kv.shape, lambda i: (0, 0))],
        out_specs=[st_spec, st_spec],
        out_shape=[st, st],
        compiler_params=_params(("parallel",)),
        name="ctx",
    )(lg, ctx, norm_g.reshape(1, d), csh.reshape(1, d), csc.reshape(1, d), w_kv)


def _rope(a, cos, sin_signed):
    return a * cos + pltpu.roll(a, RET_DK // 2, 1) * sin_signed


def _proj_body(x_ref, g_ref, sh_ref, sc_ref, cos_ref, sin_ref, w_ref,
               p_ref, q_ref, k_ref, v_ref, gz_ref, mg_ref):
    h = _rms_mod(x_ref[...], g_ref[...], sh_ref[...], sc_ref[...]).astype(BF16)

    def mm(lo, hi):
        return jnp.dot(h, w_ref[:, lo:hi], preferred_element_type=F32)

    p_ref[...] = mm(OFF_POOL, OFF_Q)
    cos = cos_ref[...]
    sin = sin_ref[...]
    qf = mm(OFF_Q, OFF_K)
    kf = mm(OFF_K, OFF_V)
    for hd in range(RET_HEADS):
        sl = slice(hd * RET_DK, (hd + 1) * RET_DK)
        q_ref[:, sl] = _rope(qf[:, sl], cos, sin).astype(BF16)
        k_ref[:, sl] = (_rope(kf[:, sl], cos, sin) * K_SCALE).astype(BF16)
    v_ref[...] = mm(OFF_V, OFF_G).astype(BF16)
    gz_ref[...] = mm(OFF_G, OFF_MERGE).astype(BF16)
    mg_ref[...] = mm(OFF_MERGE, IN_WIDTH).astype(BF16)


def proj_stage(x, norm_g, sh1, sc1, cos, sin, w_in_bf, tm=512):
    b, l, d = x.shape
    vec_b = pl.BlockSpec((None, 1, d), lambda i, j: (i, 0, 0))
    rows = lambda w: pl.BlockSpec((None, tm, w), lambda i, j: (i, j, 0))
    tab = pl.BlockSpec((tm, RET_DK), lambda i, j: (j, 0))
    outs = [(POOL_WIDTH, F32), (QK_WIDTH, BF16), (QK_WIDTH, BF16), (V_WIDTH, BF16),
            (V_WIDTH, BF16), (2 * D_MODEL, BF16)]
    return pl.pallas_call(
        _proj_body,
        grid=(b, l // tm),
        in_specs=[rows(d), pl.BlockSpec((1, d), lambda i, j: (0, 0)), vec_b, vec_b, tab, tab,
                  _resident(w_in_bf.shape)],
        out_specs=[rows(w) for w, _ in outs],
        out_shape=[jax.ShapeDtypeStruct((b, l, w), dt) for w, dt in outs],
        compiler_params=_params(("parallel", "parallel"), VMEM_LIMIT),
        name="proj",
    )(x, norm_g.reshape(1, d), sh1, sc1, cos, sin, w_in_bf)


def _ret_body(lg_ref, q_ref, k_ref, v_ref, sf_ref, sb_ref, y_ref, sfw_scr, sbw_scr, yb_scr):
    c = RET_CHUNK
    n_chunks = q_ref.shape[0] // c
    hd = pl.program_id(1)
    lgf = lg_ref[0, hd]
    lgb = lg_ref[1, hd]
    n_i = lax.broadcasted_iota(I32, (c, c), 0)
    m_i = lax.broadcasted_iota(I32, (c, c), 1)
    rel = (n_i - m_i).astype(F32)
    intra_f = jnp.where(rel >= 0, jnp.exp(lgf * jnp.where(rel >= 0, rel, 0.0)), 0.0)
    intra_b = jnp.where(rel < 0, jnp.exp(lgb * jnp.where(rel < 0, -rel, 0.0)), 0.0)
    pos = lax.broadcasted_iota(I32, (c, RET_DK), 0).astype(F32)
    qdec_f = jnp.exp(lgf * (pos + 1.0))
    kdec_f = jnp.exp(lgf * (c - 1.0 - pos))
    qdec_b = jnp.exp(lgb * (c - pos))
    kdec_b = jnp.exp(lgb * pos)
    blk_f = jnp.exp(jnp.full((1, RET_DV), lgf * c, F32))
    blk_b = jnp.exp(jnp.full((1, RET_DV), lgb * c, F32))

    def chunk(i, st_ref, intra, qdec, kdec, blk):
        rows = pl.ds(pl.multiple_of(i * c, c), c)
        qi = q_ref[rows, :]
        ki = k_ref[rows, :]
        vi = v_ref[rows, :]
        sc = lax.dot_general(qi, ki, _NT, preferred_element_type=F32) * intra
        s = st_ref[...]
        y = (jnp.dot(sc.astype(BF16), vi, preferred_element_type=F32)
             + jnp.dot((qi.astype(F32) * qdec).astype(BF16), s.astype(BF16),
                       preferred_element_type=F32))
        st_ref[...] = s * blk + lax.dot_general((ki.astype(F32) * kdec).astype(BF16), vi, _TN,
                                                preferred_element_type=F32)
        return rows, y

    sfw_scr[...] = sf_ref[...]
    sbw_scr[...] = sb_ref[...]

    def step(i, carry):
        rows, y = chunk(i, sfw_scr, intra_f, qdec_f, kdec_f, blk_f)
        y_ref[rows, :] = y
        rows, y = chunk(n_chunks - 1 - i, sbw_scr, intra_b, qdec_b, kdec_b, blk_b)
        yb_scr[rows, :] = y
        return carry

    lax.fori_loop(0, n_chunks, step, 0)
    y_ref[...] += yb_scr[...]


def ret_stage(lg, q, k, v, s_f, s_b):
    b, l, _ = q.shape
    qk_spec = pl.BlockSpec((None, l, RET_DK), lambda i, j: (i, 0, j))
    v_spec = pl.BlockSpec((None, l, RET_DV), lambda i, j: (i, 0, j))
    st_spec = pl.BlockSpec((None, None, RET_DK, RET_DV), lambda i, j: (i, j, 0, 0))
    return pl.pallas_call(
        _ret_body,
        grid=(b, RET_HEADS),
        in_specs=[pl.BlockSpec(memory_space=pltpu.SMEM), qk_spec, qk_spec, v_spec, st_spec, st_spec],
        out_specs=v_spec,
        out_shape=jax.ShapeDtypeStruct((b, l, V_WIDTH), F32),
        scratch_shapes=[pltpu.VMEM((RET_DK, RET_DV), F32), pltpu.VMEM((RET_DK, RET_DV), F32),
                        pltpu.VMEM((l, RET_DV), F32)],
        compiler_params=_params(("parallel", "parallel")),
        name="ret",
    )(lg, q, k, v, s_f, s_b)


def _pool_bands(tm):
    r = np.arange(tm)[:, None]
    c = np.arange(tm)[None, :]
    bands = np.zeros((len(POOL_WINDOWS), 3, tm, tm), np.float32)
    for gi, w in enumerate(POOL_WINDOWS):
        lo, hi = r - w // 2, r + w - w // 2
        for j, off in enumerate((-tm, 0, tm)):
            bands[gi, j] = ((c + off >= lo) & (c + off < hi)).astype(np.float32)
    return jnp.asarray(bands, BF16)


def _mix_body(pp_ref, pm_ref, pn_ref, y_ref, gz_ref, mg_ref, x_ref, g1_ref, sh2_ref, sc2_ref,
              band_ref, pw_ref, ps_ref, po_ref, rg_ref, ro_ref, wo_ref, ng_ref,
              hf_ref, x1t_ref, hft_ref, *, seq_len):
    tm = pm_ref.shape[0]
    li = pl.program_id(1)
    has_prev = (li > 0).astype(F32)
    has_next = (li < pl.num_programs(1) - 1).astype(F32)
    t = li * tm + lax.broadcasted_iota(I32, (tm, POOL_GROUP), 0)

    def window_sum(ref, cols, gi, j):
        pf = ref[:, cols]
        hi = pf.astype(BF16)
        lo = (pf - hi.astype(F32)).astype(BF16)
        band = band_ref[gi, j]
        return (jnp.dot(band, hi, preferred_element_type=F32)
                + jnp.dot(band, lo, preferred_element_type=F32))

    mixed = []
    for gi, w in enumerate(POOL_WINDOWS):
        cols = slice(gi * POOL_GROUP, (gi + 1) * POOL_GROUP)
        ws = (window_sum(pm_ref, cols, gi, 1) + has_prev * window_sum(pp_ref, cols, gi, 0)
              + has_next * window_sum(pn_ref, cols, gi, 2))
        cnt = (jnp.clip(t + (w - w // 2), 0, seq_len) - jnp.clip(t - w // 2, 0, seq_len)).astype(F32)
        dgi = ws / cnt - pm_ref[:, cols]
        mixed.append(jnp.dot(dgi.astype(BF16), pw_ref[gi], preferred_element_type=F32))
    mixed = jnp.concatenate(mixed, axis=1) * ps_ref[...]
    pool = jnp.dot(mixed.astype(BF16), po_ref[...], preferred_element_type=F32)

    yn = []
    for hd in range(RET_HEADS):
        yh = y_ref[:, hd * RET_DV:(hd + 1) * RET_DV]
        mu = jnp.mean(yh, axis=-1, keepdims=True)
        yc = yh - mu
        var = jnp.mean(yc * yc, axis=-1, keepdims=True)
        yn.append(yc * lax.rsqrt(var + EPS))
    yn = jnp.concatenate(yn, axis=1) * rg_ref[...]
    gate = gz_ref[...].astype(F32)
    ret = jnp.dot((yn * (gate * jax.nn.sigmoid(gate))).astype(BF16), ro_ref[...],
                  preferred_element_type=F32)

    g_pool = mg_ref[:, :D_MODEL].astype(F32)
    g_ret = mg_ref[:, D_MODEL:].astype(F32)
    merged = jax.nn.sigmoid(g_pool) * pool + jax.nn.sigmoid(g_ret) * ret
    out = jnp.dot(merged.astype(BF16), wo_ref[...], preferred_element_type=F32)
    x1 = x_ref[...] + g1_ref[...] * out
    hf = _rms_mod(x1, ng_ref[...], sh2_ref[...], sc2_ref[...])
    hf_ref[...] = hf.astype(BF16)
    for r in range(tm // SUBLANES):
        for c in range(ROW_TILES):
            dst = pl.ds(r * SUBLANES * ROW_TILES + c, SUBLANES, stride=ROW_TILES)
            src = (slice(r * SUBLANES, (r + 1) * SUBLANES), slice(c * LANES, (c + 1) * LANES))
            x1t_ref[dst, :] = x1[src]
            hft_ref[dst, :] = hf[src]


def mix_stage(p, y, gz, mg, x, g1, sh2, sc2, pool_w_bf, pool_scale, pool_out_bf, ret_norm_g,
              ret_out_bf, w_out_bf, norm_ffn_g, tm=256):
    b, l, d = x.shape
    nl = l // tm
    bands = _pool_bands(tm)
    rows = lambda w: pl.BlockSpec((None, tm, w), lambda i, j: (i, j, 0))
    vec_b = pl.BlockSpec((None, 1, d), lambda i, j: (i, 0, 0))
    const = lambda a: pl.BlockSpec(a.shape, lambda i, j: (0,) * a.ndim)
    tiles = pl.BlockSpec((tm * ROW_TILES, LANES), lambda i, j: (i * nl + j, 0))
    ps = pool_scale.reshape(1, POOL_WIDTH)
    rg = ret_norm_g.reshape(1, V_WIDTH)
    ng = norm_ffn_g.reshape(1, d)
    return pl.pallas_call(
        functools.partial(_mix_body, seq_len=l),
        grid=(b, nl),
        in_specs=[pl.BlockSpec((None, tm, POOL_WIDTH), lambda i, j: (i, jnp.maximum(j - 1, 0), 0)),
                  rows(POOL_WIDTH),
                  pl.BlockSpec((None, tm, POOL_WIDTH), lambda i, j: (i, jnp.minimum(j + 1, nl - 1), 0)),
                  rows(V_WIDTH), rows(V_WIDTH), rows(2 * D_MODEL), rows(d), vec_b, vec_b, vec_b,
                  const(bands), const(pool_w_bf), const(ps), const(pool_out_bf), const(rg),
                  const(ret_out_bf), const(w_out_bf), const(ng)],
        out_specs=[rows(d), tiles, tiles],
        out_shape=[jax.ShapeDtypeStruct((b, l, d), BF16),
                   jax.ShapeDtypeStruct((b * l * ROW_TILES, LANES), F32),
                   jax.ShapeDtypeStruct((b * l * ROW_TILES, LANES), F32)],
        compiler_params=_params(("parallel", "parallel"), VMEM_LIMIT),
        name="mix",
    )(p, p, p, y, gz, mg, x, g1, sh2, sc2, bands, pool_w_bf, ps, pool_out_bf, rg, ret_out_bf,
      w_out_bf, ng)


def _topk_rows(s, k):
    r, n = s.shape
    sub = lax.broadcasted_iota(I32, (SUBLANES, n), 0)
    tiles = [s[v * SUBLANES:(v + 1) * SUBLANES, :] for v in range(r // SUBLANES)]
    rows = [sub + v * SUBLANES for v in range(r // SUBLANES)]
    slot = lax.broadcasted_iota(I32, (k, n), 0)
    vals = jnp.zeros((k, n), F32)
    idxs = jnp.zeros((k, n), I32)
    for j in range(k):
        level = list(zip(tiles, rows))
        while len(level) > 1:
            merged = [(jnp.maximum(va, vb), jnp.where(vb > va, ib, ia))
                      for (va, ia), (vb, ib) in zip(level[0::2], level[1::2])]
            level = merged + level[len(level) & ~1:]
        v8, i8 = level[0]
        m = jnp.max(v8, axis=0, keepdims=True)
        i = jnp.min(jnp.where(v8 == m, i8, r), axis=0, keepdims=True)
        vals = jnp.where(slot == j, m, vals)
        idxs = jnp.where(slot == j, i, idxs)
        tiles = [jnp.where(ri == i, -jnp.inf, t) for t, ri in zip(tiles, rows)]
    return vals, idxs


def _pick_rows(table, sel):
    out = jnp.zeros(sel.shape, table.dtype)
    for r in range(table.shape[0]):
        out = jnp.where(sel == r, table[r:r + 1, :], out)
    return out


_HALF_K = PEER_TOPK // 2
assert PEER_TOPK == 16 and SUBLANES == _HALF_K
_CAND_MID = PEER_TOPK + (_HALF_K - 1) * _HALF_K


def _candidates(s1, s2):
    parts = [s1[0:1, :] + s2]
    parts += [s1[i:i + 1, :] + s2[0:_HALF_K, :] for i in range(1, _HALF_K)]
    parts.append(s1[_HALF_K:, :] + s2[0:1, :])
    return jnp.concatenate(parts, axis=0)


def _candidate_ij(pos):
    mid = pos - PEER_TOPK
    i = jnp.where(pos < PEER_TOPK, 0, jnp.where(pos < _CAND_MID, (mid >> 3) + 1, pos - (_CAND_MID - _HALF_K)))
    j = jnp.where(pos < PEER_TOPK, pos, jnp.where(pos < _CAND_MID, mid & (_HALF_K - 1), 0))
    return i, j


def _route_body(hf_ref, wq_ref, keys_ref, e_ref, g_ref):
    tq = hf_ref.shape[0]
    half = PEER_DQ // 2
    q = jnp.dot(hf_ref[...], wq_ref[...], preferred_element_type=F32).astype(BF16)
    for cb in range(tq // LANES):
        tok = slice(cb * LANES, (cb + 1) * LANES)
        e_heads, g_heads = [], []
        for hd in range(PEER_HEADS):
            sub = []
            for part in range(2):
                col = (hd * 2 + part) * half
                st = lax.dot_general(keys_ref[hd, part], q[tok, col:col + half], _NT,
                                     preferred_element_type=F32)
                sub.append(_topk_rows(st, PEER_TOPK))
            (s1, i1), (s2, i2) = sub
            best, pos = _topk_rows(_candidates(s1, s2), PEER_TOPK)
            ci, cj = _candidate_ij(pos)
            e1 = _pick_rows(i1, ci)
            e2 = _pick_rows(i2, cj)
            ex = jnp.exp(best - best[0:1, :])
            e_heads.append((e1 * PEER_NKEYS + e2) * PACKED_ROWS)
            g_heads.append(ex / jnp.sum(ex, axis=0, keepdims=True))
        e_ref[tok, :] = jnp.concatenate(e_heads, axis=0).T
        g_ref[tok, :] = jnp.concatenate(g_heads, axis=0).T


def route_stage(hf, wq_bf, keys_bf, tq=256):
    n, d = hf.shape
    rows = lambda w: pl.BlockSpec((tq, w), lambda i: (i, 0))
    return pl.pallas_call(
        _route_body,
        grid=(n // tq,),
        in_specs=[rows(d), pl.BlockSpec(wq_bf.shape, lambda i: (0, 0)),
                  pl.BlockSpec(keys_bf.shape, lambda i: (0, 0, 0, 0))],
        out_specs=[rows(PEER_SLOTS), rows(PEER_SLOTS)],
        out_shape=[jax.ShapeDtypeStruct((n, PEER_SLOTS), I32),
                   jax.ShapeDtypeStruct((n, PEER_SLOTS), F32)],
        compiler_params=_params(("parallel",), VMEM_LIMIT),
        name="route",
    )(hf, wq_bf, keys_bf)


def _pack_table(t):
    tb = t.astype(BF16).reshape(PEER_EXPERTS, PACKED_ROWS, 2, LANES).transpose(0, 1, 3, 2)
    return lax.bitcast_convert_type(tb, jnp.uint32).reshape(PEER_EXPERTS * PACKED_ROWS, LANES)


def _table_tile(words):
    return pltpu.bitcast(words, BF16).astype(F32)


OFF_BITS = 16
assert PEER_EXPERTS * PACKED_ROWS <= 1 << OFF_BITS
PEER_TB = 128
_HALF_TB = PEER_TB // 2
_HALF_WORDS = _HALF_TB * PEER_SLOTS


def _offset_copy(off_hbm, bufs, sems, step, h):
    start = (2 * step + h) * _HALF_WORDS
    return pltpu.make_async_copy(off_hbm.at[pl.ds(start, _HALF_WORDS)], bufs[h], sems.at[h])


def _for_each_half(off_hbm, bufs, sems, half_fn):
    step = pl.program_id(0)

    @pl.when(step == 0)
    def _():
        _offset_copy(off_hbm, bufs, sems, step, 0).start()

    _offset_copy(off_hbm, bufs, sems, step, 1).start()
    _offset_copy(off_hbm, bufs, sems, step, 0).wait()
    half_fn(bufs[0], 0)

    @pl.when(step + 1 < pl.num_programs(0))
    def _():
        _offset_copy(off_hbm, bufs, sems, step + 1, 0).start()

    _offset_copy(off_hbm, bufs, sems, step, 1).wait()
    half_fn(bufs[1], _HALF_TB)


def _gather_row(tbl_ref, off):
    return tbl_ref[pl.ds(pl.multiple_of(off, PACKED_ROWS), PACKED_ROWS), :]


_OFFSET_SCRATCH = [pltpu.SMEM((_HALF_WORDS,), I32), pltpu.SMEM((_HALF_WORDS,), I32),
                   pltpu.SemaphoreType.DMA((2,))]


def _peer_u_body(off_hbm, h_ref, g_ref, tbl_ref, o_ref, idx_a, idx_b, sems, s_scr):
    eye = (lax.broadcasted_iota(I32, (PEER_SLOTS, LANES), 0)
           == lax.broadcasted_iota(I32, (PEER_SLOTS, LANES), 1))
    octet = SUBLANES * SUBLANES
    sel = (lax.broadcasted_iota(I32, (SUBLANES, octet), 1) // SUBLANES
           == lax.broadcasted_iota(I32, (SUBLANES, octet), 0)).astype(BF16)

    def half(idx_ref, first):
        for t in range(_HALF_TB):
            tok = first + t
            hv = h_ref[pl.ds(tok * ROW_TILES, ROW_TILES), :]
            for j in range(PEER_SLOTS // SUBLANES):
                prods = [_table_tile(_gather_row(tbl_ref, idx_ref[t * PEER_SLOTS + k])) * hv
                         for k in range(j * SUBLANES, (j + 1) * SUBLANES)]
                stack = jnp.concatenate(prods, axis=0)
                acc = jnp.dot(sel, stack.astype(BF16), preferred_element_type=F32)
                s_scr[pl.ds(tok * PEER_SLOTS + j * SUBLANES, SUBLANES), :] = acc
        for r0 in range(first, first + _HALF_TB, SUBLANES):
            acts = []
            for i in range(SUBLANES):
                tot = jnp.sum(s_scr[pl.ds((r0 + i) * PEER_SLOTS, PEER_SLOTS), :], axis=1, keepdims=True)
                acts.append(jnp.sum(jnp.where(eye, tot, 0.0), axis=0, keepdims=True))
            act = jnp.concatenate(acts, axis=0)
            rows8 = pl.ds(r0, SUBLANES)
            o_ref[rows8, :] = g_ref[rows8, :] * (0.5 * act * (1.0 + lax.erf(act * (2.0 ** -0.5))))

    _for_each_half(off_hbm, (idx_a, idx_b), sems, half)


def _token_tiles():
    return pl.BlockSpec((PEER_TB * ROW_TILES, LANES), lambda i: (i, 0))


def peer_u_stage(eoff, hf_tiles, gates, tbl):
    n = eoff.shape[0]
    rows = lambda: pl.BlockSpec((PEER_TB, PEER_SLOTS), lambda i: (i, 0))
    return pl.pallas_call(
        _peer_u_body,
        grid=(n // PEER_TB,),
        in_specs=[pl.BlockSpec(memory_space=pl.ANY), _token_tiles(), rows(), _resident(tbl.shape)],
        out_specs=rows(),
        out_shape=jax.ShapeDtypeStruct((n, PEER_SLOTS), F32),
        scratch_shapes=_OFFSET_SCRATCH + [pltpu.VMEM((PEER_TB * PEER_SLOTS, LANES), F32)],
        compiler_params=_params(("arbitrary",), VMEM_LIMIT),
        name="peer_u",
    )(eoff.reshape(-1), hf_tiles, gates, tbl)


_V_ACCS = 4


def _peer_v_body(off_hbm, w_ref, x_ref, g2_ref, fg_ref, tbl_ref, o_ref, idx_a, idx_b, sems, acc_scr):
    def half(idx_ref, first):
        for t in range(_HALF_TB):
            tok = first + t
            wrep = jnp.broadcast_to(w_ref[tok], (LANES, PEER_SLOTS)).T
            accs = [None] * _V_ACCS
            for k in range(PEER_SLOTS):
                row = _table_tile(_gather_row(tbl_ref, idx_ref[t * PEER_SLOTS + k]))
                term = jnp.broadcast_to(wrep[k:k + 1, :], (ROW_TILES, LANES)) * row
                accs[k % _V_ACCS] = term if accs[k % _V_ACCS] is None else accs[k % _V_ACCS] + term
            acc_scr[pl.ds(tok * ROW_TILES, ROW_TILES), :] = (accs[0] + accs[1]) + (accs[2] + accs[3])
        rows = pl.ds(first, _HALF_TB)
        x2 = []
        for c in range(ROW_TILES):
            block = pl.ds(first * ROW_TILES + c, _HALF_TB, stride=ROW_TILES)
            x2.append(x_ref[block, :] + g2_ref[c:c + 1, :] * acc_scr[block, :])
        ms = sum(jnp.sum(v * v, axis=1, keepdims=True) for v in x2) / D_MODEL
        inv = lax.rsqrt(ms + EPS)
        for c in range(ROW_TILES):
            o_ref[rows, c * LANES:(c + 1) * LANES] = (x2[c] * inv) * fg_ref[c:c + 1, :]

    _for_each_half(off_hbm, (idx_a, idx_b), sems, half)


def peer_v_stage(eoff, w, x_tiles, g2, final_g, tbl, tokens_per_batch):
    n = eoff.shape[0]
    per_b = tokens_per_batch // PEER_TB
    return pl.pallas_call(
        _peer_v_body,
        grid=(n // PEER_TB,),
        in_specs=[pl.BlockSpec(memory_space=pl.ANY),
                  pl.BlockSpec((PEER_TB, 1, PEER_SLOTS), lambda i: (i, 0, 0)), _token_tiles(),
                  pl.BlockSpec((None, ROW_TILES, LANES), lambda i: (i // per_b, 0, 0)),
                  pl.BlockSpec((ROW_TILES, LANES), lambda i: (0, 0)),
                  _resident(tbl.shape)],
        out_specs=pl.BlockSpec((PEER_TB, D_MODEL), lambda i: (i, 0)),
        out_shape=jax.ShapeDtypeStruct((n, D_MODEL), F32),
        scratch_shapes=_OFFSET_SCRATCH + [pltpu.VMEM((PEER_TB * ROW_TILES, LANES), F32)],
        compiler_params=_params(("arbitrary",), VMEM_LIMIT),
        name="peer_v",
    )(eoff.reshape(-1), w.reshape(n, 1, PEER_SLOTS), x_tiles, g2, final_g.reshape(ROW_TILES, LANES), tbl)


def _rope_tables(l):
    quarter = RET_DK // 4
    rows = l // GRID_W
    row = jnp.repeat(jnp.arange(rows, dtype=F32), GRID_W)
    col = jnp.tile(jnp.arange(GRID_W, dtype=F32), rows)
    inv = ROPE_BASE ** (-jnp.arange(quarter, dtype=F32) / quarter)
    ang = jnp.concatenate([row[:, None] * inv, col[:, None] * inv], axis=-1)
    cos, sin = jnp.cos(ang), jnp.sin(ang)
    return jnp.concatenate([cos, cos], axis=-1), jnp.concatenate([-sin, sin], axis=-1)


def _layer(x, ctx, c, c_ctx, ada_w, ada_b, norm_mix_g, norm_ffn_g, w_in, pool_w, pool_scale,
           pool_out, ret_decay, ret_norm_g, ret_out, w_out, peer_wq, peer_keys, peer_u, peer_v,
           final_g):
    b, l, d = x.shape
    n = b * l

    rows = -(-(b + 1) // SUBLANES) * SUBLANES
    cc = jnp.zeros((rows, d), F32).at[:b].set(c).at[b].set(c_ctx)
    mod = ada_stage(cc, ada_w, ada_b)
    sh1, sc1, g1, sh2, sc2, g2 = [m.reshape(b, 1, d) for m in jnp.split(mod[:b], 6, axis=-1)]
    csh1, csc1 = mod[b, :d], mod[b, d:2 * d]
    lg = jax.nn.log_sigmoid(ret_decay.astype(F32))

    w_in_bf = w_in.astype(BF16)
    s_f, s_b = ctx_stage(lg, ctx, norm_mix_g, csh1, csc1, w_in_bf[:, OFF_K:OFF_G])

    cos, sin = _rope_tables(l)
    p, q, k, v, gz, mg = proj_stage(x, norm_mix_g, sh1, sc1, cos, sin, w_in_bf)
    y = ret_stage(lg, q, k, v, s_f, s_b)
    hf_bf, x1_tiles, hf_tiles = mix_stage(p, y, gz, mg, x, g1, sh2, sc2, pool_w.astype(BF16), pool_scale,
                                          pool_out.astype(BF16), ret_norm_g, ret_out.astype(BF16),
                                          w_out.astype(BF16), norm_ffn_g)

    eoff, gates = route_stage(hf_bf.reshape(n, d), peer_wq.astype(BF16), peer_keys.astype(BF16))
    w = peer_u_stage(eoff, hf_tiles, gates, _pack_table(peer_u))
    out = peer_v_stage(eoff, w, x1_tiles, g2.reshape(b, ROW_TILES, LANES), final_g, _pack_table(peer_v), l)
    return out.reshape(b, l, d)


def kernel(x, c, ctx, c_ctx, ada_w, ada_b, norm_mix_g, norm_ffn_g, w_in, pool_w, pool_scale, pool_out, ret_decay, ret_norm_g, ret_out, w_out, peer_wq, peer_keys, peer_u, peer_v, final_g):
    assert ada_w.shape[0] == 1, "single-layer block"
    return _layer(x, ctx, c, c_ctx, ada_w[0], ada_b[0], norm_mix_g[0], norm_ffn_g[0], w_in[0],
                  pool_w[0], pool_scale[0], pool_out[0], ret_decay[0], ret_norm_g[0], ret_out[0],
                  w_out[0], peer_wq[0], peer_keys[0], peer_u[0], peer_v[0], final_g)
```

```python
import functools

import jax
import jax.numpy as jnp
import numpy as np
from jax import lax
from jax.experimental import pallas as pl
from jax.experimental.pallas import tpu as pltpu

F32 = jnp.float32
BF16 = jnp.bfloat16
I32 = jnp.int32

D_MODEL = 1024
GRID_W = 64
EPS = 1e-6

POOL_WINDOWS = (2, 4, 8, 16)
POOL_WIDTH = D_MODEL // 2
POOL_GROUP = POOL_WIDTH // len(POOL_WINDOWS)

RET_HEADS = 4
RET_DK = 128
RET_DV = 256
RET_CHUNK = 128
ROPE_BASE = 10000.0
QK_WIDTH = RET_HEADS * RET_DK
V_WIDTH = RET_HEADS * RET_DV
K_SCALE = RET_DK ** -0.5

OFF_POOL = 0
OFF_Q = OFF_POOL + POOL_WIDTH
OFF_K = OFF_Q + QK_WIDTH
OFF_V = OFF_K + QK_WIDTH
OFF_G = OFF_V + V_WIDTH
OFF_MERGE = OFF_G + V_WIDTH
IN_WIDTH = OFF_MERGE + 2 * D_MODEL

PEER_HEADS = 8
PEER_NKEYS = 128
PEER_EXPERTS = PEER_NKEYS * PEER_NKEYS
PEER_TOPK = 16
PEER_DQ = 256
PEER_SLOTS = PEER_HEADS * PEER_TOPK

LANES = 128
SUBLANES = 8
ROW_TILES = D_MODEL // LANES
PACKED_ROWS = ROW_TILES // 2
VMEM_LIMIT = 56 * 1024 * 1024

_NT = (((1,), (1,)), ((), ()))
_TN = (((0,), (0,)), ((), ()))


def _params(sem, vmem=None):
    return pltpu.CompilerParams(dimension_semantics=sem, vmem_limit_bytes=vmem)


def _resident(shape):
    nd = len(shape)
    return pl.BlockSpec(shape, lambda *_: (0,) * nd, pipeline_mode=pl.Buffered(1))


def _rms_mod(xf, g, shift, scale):
    y = xf * lax.rsqrt(jnp.mean(xf * xf, axis=-1, keepdims=True) + EPS)
    return (y * g) * (1.0 + scale) + shift


def _ada_body(c_ref, w_ref, b_ref, o_ref):
    c = c_ref[...]
    s = c * jax.nn.sigmoid(c)
    o_ref[...] = jnp.dot(s, w_ref[...], preferred_element_type=F32,
                         precision=lax.Precision.HIGHEST) + b_ref[...]


def ada_stage(cc, ada_w, ada_b, tn=512):
    r, d = cc.shape
    n = ada_w.shape[1]
    return pl.pallas_call(
        _ada_body,
        grid=(n // tn,),
        in_specs=[pl.BlockSpec((r, d), lambda j: (0, 0)),
                  pl.BlockSpec((d, tn), lambda j: (0, j)),
                  pl.BlockSpec((1, tn), lambda j: (0, j))],
        out_specs=pl.BlockSpec((r, tn), lambda j: (0, j)),
        out_shape=jax.ShapeDtypeStruct((r, n), F32),
        compiler_params=_params(("parallel",)),
        name="ada",
    )(cc, ada_w, ada_b.reshape(1, n))


def _ctx_body(lg_ref, ctx_ref, g_ref, sh_ref, sc_ref, w_ref, sf_ref, sb_ref):
    lc = ctx_ref.shape[0]
    hc = _rms_mod(ctx_ref[...], g_ref[...], sh_ref[...], sc_ref[...])
    kv = jnp.dot(hc.astype(BF16), w_ref[...], preferred_element_type=F32)
    m = lax.broadcasted_iota(I32, (lc, RET_DK), 0).astype(F32)
    for h in range(RET_HEADS):
        kf = kv[:, h * RET_DK:(h + 1) * RET_DK] * K_SCALE
        vb = kv[:, QK_WIDTH + h * RET_DV:QK_WIDTH + (h + 1) * RET_DV].astype(BF16)
        wf = jnp.exp(lg_ref[0, h] * (lc - 1.0 - m))
        wb = jnp.exp(lg_ref[1, h] * m)
        sf_ref[h] = lax.dot_general((kf * wf).astype(BF16), vb, _TN, preferred_element_type=F32)
        sb_ref[h] = lax.dot_general((kf * wb).astype(BF16), vb, _TN, preferred_element_type=F32)


def ctx_stage(lg, ctx, norm_g, csh, csc, w_kv):
    b, lc, d = ctx.shape
    vec = pl.BlockSpec((1, d), lambda i: (0, 0))
    st = jax.ShapeDtypeStruct((b, RET_HEADS, RET_DK, RET_DV), F32)
    st_spec = pl.BlockSpec((None, RET_HEADS, RET_DK, RET_DV), lambda i: (i, 0, 0, 0))
    return pl.pallas_call(
        _ctx_body,
        grid=(b,),
        in_specs=[pl.BlockSpec(memory_space=pltpu.SMEM),
                  pl.BlockSpec((None, lc, d), lambda i: (i, 0, 0)),
                  vec, vec, vec,
                  pl.BlockSpec(w_kv.shape, lambda i: (0, 0))],
        out_specs=[st_spec, st_spec],
        out_shape=[st, st],
        compiler_params=_params(("parallel",)),
        name="ctx",
    )(lg, ctx, norm_g.reshape(1, d), csh.reshape(1, d), csc.reshape(1, d), w_kv)


def _rope(a, cos, sin_signed):
    return a * cos + pltpu.roll(a, RET_DK // 2, 1) * sin_signed


def _proj_body(x_ref, g_ref, sh_ref, sc_ref, cos_ref, sin_ref, w_ref,
               p_ref, q_ref, k_ref, v_ref, gz_ref, mg_ref):
    h = _rms_mod(x_ref[...], g_ref[...], sh_ref[...], sc_ref[...]).astype(BF16)

    def mm(lo, hi):
        return jnp.dot(h, w_ref[:, lo:hi], preferred_element_type=F32)

    p_ref[...] = mm(OFF_POOL, OFF_Q)
    cos = cos_ref[...]
    sin = sin_ref[...]
    qf = mm(OFF_Q, OFF_K)
    kf = mm(OFF_K, OFF_V)
    for hd in range(RET_HEADS):
        sl = slice(hd * RET_DK, (hd + 1) * RET_DK)
        q_ref[:, sl] = _rope(qf[:, sl], cos, sin).astype(BF16)
        k_ref[:, sl] = (_rope(kf[:, sl], cos, sin) * K_SCALE).astype(BF16)
    v_ref[...] = mm(OFF_V, OFF_G).astype(BF16)
    gz_ref[...] = mm(OFF_G, OFF_MERGE).astype(BF16)
    mg_ref[...] = mm(OFF_MERGE, IN_WIDTH).astype(BF16)


def proj_stage(x, norm_g, sh1, sc1, cos, sin, w_in_bf, tm=512):
    b, l, d = x.shape
    vec_b = pl.BlockSpec((None, 1, d), lambda i, j: (i, 0, 0))
    rows = lambda w: pl.BlockSpec((None, tm, w), lambda i, j: (i, j, 0))
    tab = pl.BlockSpec((tm, RET_DK), lambda i, j: (j, 0))
    outs = [(POOL_WIDTH, F32), (QK_WIDTH, BF16), (QK_WIDTH, BF16), (V_WIDTH, BF16),
            (V_WIDTH, BF16), (2 * D_MODEL, BF16)]
    return pl.pallas_call(
        _proj_body,
        grid=(b, l // tm),
        in_specs=[rows(d), pl.BlockSpec((1, d), lambda i, j: (0, 0)), vec_b, vec_b, tab, tab,
                  _resident(w_in_bf.shape)],
        out_specs=[rows(w) for w, _ in outs],
        out_shape=[jax.ShapeDtypeStruct((b, l, w), dt) for w, dt in outs],
        compiler_params=_params(("parallel", "parallel"), VMEM_LIMIT),
        name="proj",
    )(x, norm_g.reshape(1, d), sh1, sc1, cos, sin, w_in_bf)


_RET_GROUP = 2


def _ret_body(lg_ref, q_ref, k_ref, v_ref, sf_ref, sb_ref, y_ref, sfw_scr, sbw_scr, yb_scr):
    c = RET_CHUNK
    n_chunks = q_ref.shape[0] // c
    n_i = lax.broadcasted_iota(I32, (c, c), 0)
    m_i = lax.broadcasted_iota(I32, (c, c), 1)
    rel = (n_i - m_i).astype(F32)
    pos = lax.broadcasted_iota(I32, (c, RET_DK), 0).astype(F32)

    def decays(hh):
        hd = pl.program_id(1) * _RET_GROUP + hh
        lgf = lg_ref[0, hd]
        lgb = lg_ref[1, hd]
        fwd = (jnp.where(rel >= 0, jnp.exp(lgf * jnp.where(rel >= 0, rel, 0.0)), 0.0),
               jnp.exp(lgf * (pos + 1.0)), jnp.exp(lgf * (c - 1.0 - pos)),
               jnp.exp(jnp.full((1, RET_DV), lgf * c, F32)))
        bwd = (jnp.where(rel < 0, jnp.exp(lgb * jnp.where(rel < 0, -rel, 0.0)), 0.0),
               jnp.exp(lgb * (c - pos)), jnp.exp(lgb * pos),
               jnp.exp(jnp.full((1, RET_DV), lgb * c, F32)))
        return fwd, bwd

    tables = [decays(hh) for hh in range(_RET_GROUP)]

    def chunk(i, hh, st_ref, intra, qdec, kdec, blk):
        rows = pl.ds(pl.multiple_of(i * c, c), c)
        qk_cols = slice(hh * RET_DK, (hh + 1) * RET_DK)
        v_cols = slice(hh * RET_DV, (hh + 1) * RET_DV)
        qi = q_ref[rows, qk_cols]
        ki = k_ref[rows, qk_cols]
        vi = v_ref[rows, v_cols]
        sc = lax.dot_general(qi, ki, _NT, preferred_element_type=F32) * intra
        s = st_ref[hh]
        y = (jnp.dot(sc.astype(BF16), vi, preferred_element_type=F32)
             + jnp.dot((qi.astype(F32) * qdec).astype(BF16), s.astype(BF16),
                       preferred_element_type=F32))
        st_ref[hh] = s * blk + lax.dot_general((ki.astype(F32) * kdec).astype(BF16), vi, _TN,
                                               preferred_element_type=F32)
        return rows, v_cols, y

    sfw_scr[...] = sf_ref[...]
    sbw_scr[...] = sb_ref[...]

    def step(i, carry):
        for hh, (fwd, bwd) in enumerate(tables):
            rows, cols, y = chunk(i, hh, sfw_scr, *fwd)
            y_ref[rows, cols] = y
            rows, cols, y = chunk(n_chunks - 1 - i, hh, sbw_scr, *bwd)
            yb_scr[rows, cols] = y
        return carry

    lax.fori_loop(0, n_chunks, step, 0)
    y_ref[...] += yb_scr[...]


def ret_stage(lg, q, k, v, s_f, s_b):
    b, l, _ = q.shape
    g = _RET_GROUP
    qk_spec = pl.BlockSpec((None, l, g * RET_DK), lambda i, j: (i, 0, j))
    v_spec = pl.BlockSpec((None, l, g * RET_DV), lambda i, j: (i, 0, j))
    st_spec = pl.BlockSpec((None, g, RET_DK, RET_DV), lambda i, j: (i, j, 0, 0))
    return pl.pallas_call(
        _ret_body,
        grid=(b, RET_HEADS // g),
        in_specs=[pl.BlockSpec(memory_space=pltpu.SMEM), qk_spec, qk_spec, v_spec, st_spec, st_spec],
        out_specs=v_spec,
        out_shape=jax.ShapeDtypeStruct((b, l, V_WIDTH), F32),
        scratch_shapes=[pltpu.VMEM((g, RET_DK, RET_DV), F32), pltpu.VMEM((g, RET_DK, RET_DV), F32),
                        pltpu.VMEM((l, g * RET_DV), F32)],
        compiler_params=_params(("parallel", "parallel")),
        name="ret",
    )(lg, q, k, v, s_f, s_b)


def _pool_bands(tm):
    r = np.arange(tm)[:, None]
    c = np.arange(tm)[None, :]
    bands = np.zeros((len(POOL_WINDOWS), 3, tm, tm), np.float32)
    for gi, w in enumerate(POOL_WINDOWS):
        lo, hi = r - w // 2, r + w - w // 2
        for j, off in enumerate((-tm, 0, tm)):
            bands[gi, j] = ((c + off >= lo) & (c + off < hi)).astype(np.float32)
    return jnp.asarray(bands, BF16)


def _mix_body(pp_ref, pm_ref, pn_ref, y_ref, gz_ref, mg_ref, x_ref, g1_ref, sh2_ref, sc2_ref,
              band_ref, pw_ref, ps_ref, po_ref, rg_ref, ro_ref, wo_ref, ng_ref,
              hf_ref, x1t_ref, hft_ref, *, seq_len):
    tm = pm_ref.shape[0]
    li = pl.program_id(1)
    has_prev = (li > 0).astype(F32)
    has_next = (li < pl.num_programs(1) - 1).astype(F32)
    t = li * tm + lax.broadcasted_iota(I32, (tm, POOL_GROUP), 0)

    def window_sum(ref, cols, gi, j):
        pf = ref[:, cols]
        hi = pf.astype(BF16)
        lo = (pf - hi.astype(F32)).astype(BF16)
        band = band_ref[gi, j]
        return (jnp.dot(band, hi, preferred_element_type=F32)
                + jnp.dot(band, lo, preferred_element_type=F32))

    mixed = []
    for gi, w in enumerate(POOL_WINDOWS):
        cols = slice(gi * POOL_GROUP, (gi + 1) * POOL_GROUP)
        ws = (window_sum(pm_ref, cols, gi, 1) + has_prev * window_sum(pp_ref, cols, gi, 0)
              + has_next * window_sum(pn_ref, cols, gi, 2))
        cnt = (jnp.clip(t + (w - w // 2), 0, seq_len) - jnp.clip(t - w // 2, 0, seq_len)).astype(F32)
        dgi = ws / cnt - pm_ref[:, cols]
        mixed.append(jnp.dot(dgi.astype(BF16), pw_ref[gi], preferred_element_type=F32))
    mixed = jnp.concatenate(mixed, axis=1) * ps_ref[...]
    pool = jnp.dot(mixed.astype(BF16), po_ref[...], preferred_element_type=F32)

    yn = []
    for hd in range(RET_HEADS):
        yh = y_ref[:, hd * RET_DV:(hd + 1) * RET_DV]
        mu = jnp.mean(yh, axis=-1, keepdims=True)
        yc = yh - mu
        var = jnp.mean(yc * yc, axis=-1, keepdims=True)
        yn.append(yc * lax.rsqrt(var + EPS))
    yn = jnp.concatenate(yn, axis=1) * rg_ref[...]
    gate = gz_ref[...].astype(F32)
    ret = jnp.dot((yn * (gate * jax.nn.sigmoid(gate))).astype(BF16), ro_ref[...],
                  preferred_element_type=F32)

    g_pool = mg_ref[:, :D_MODEL].astype(F32)
    g_ret = mg_ref[:, D_MODEL:].astype(F32)
    merged = jax.nn.sigmoid(g_pool) * pool + jax.nn.sigmoid(g_ret) * ret
    out = jnp.dot(merged.astype(BF16), wo_ref[...], preferred_element_type=F32)
    x1 = x_ref[...] + g1_ref[...] * out
    hf = _rms_mod(x1, ng_ref[...], sh2_ref[...], sc2_ref[...])
    hf_ref[...] = hf.astype(BF16)
    for r in range(tm // SUBLANES):
        for c in range(ROW_TILES):
            dst = pl.ds(r * SUBLANES * ROW_TILES + c, SUBLANES, stride=ROW_TILES)
            src = (slice(r * SUBLANES, (r + 1) * SUBLANES), slice(c * LANES, (c + 1) * LANES))
            x1t_ref[dst, :] = x1[src]
            hft_ref[dst, :] = hf[src]


def mix_stage(p, y, gz, mg, x, g1, sh2, sc2, pool_w_bf, pool_scale, pool_out_bf, ret_norm_g,
              ret_out_bf, w_out_bf, norm_ffn_g, tm=256):
    b, l, d = x.shape
    nl = l // tm
    bands = _pool_bands(tm)
    rows = lambda w: pl.BlockSpec((None, tm, w), lambda i, j: (i, j, 0))
    vec_b = pl.BlockSpec((None, 1, d), lambda i, j: (i, 0, 0))
    const = lambda a: pl.BlockSpec(a.shape, lambda i, j: (0,) * a.ndim)
    tiles = pl.BlockSpec((tm * ROW_TILES, LANES), lambda i, j: (i * nl + j, 0))
    ps = pool_scale.reshape(1, POOL_WIDTH)
    rg = ret_norm_g.reshape(1, V_WIDTH)
    ng = norm_ffn_g.reshape(1, d)
    return pl.pallas_call(
        functools.partial(_mix_body, seq_len=l),
        grid=(b, nl),
        in_specs=[pl.BlockSpec((None, tm, POOL_WIDTH), lambda i, j: (i, jnp.maximum(j - 1, 0), 0)),
                  rows(POOL_WIDTH),
                  pl.BlockSpec((None, tm, POOL_WIDTH), lambda i, j: (i, jnp.minimum(j + 1, nl - 1), 0)),
                  rows(V_WIDTH), rows(V_WIDTH), rows(2 * D_MODEL), rows(d), vec_b, vec_b, vec_b,
                  const(bands), const(pool_w_bf), const(ps), const(pool_out_bf), const(rg),
                  const(ret_out_bf), const(w_out_bf), const(ng)],
        out_specs=[rows(d), tiles, tiles],
        out_shape=[jax.ShapeDtypeStruct((b, l, d), BF16),
                   jax.ShapeDtypeStruct((b * l * ROW_TILES, LANES), F32),
                   jax.ShapeDtypeStruct((b * l * ROW_TILES, LANES), F32)],
        compiler_params=_params(("parallel", "parallel"), VMEM_LIMIT),
        name="mix",
    )(p, p, p, y, gz, mg, x, g1, sh2, sc2, bands, pool_w_bf, ps, pool_out_bf, rg, ret_out_bf,
      w_out_bf, ng)


def _topk_rows(s, k):
    r, n = s.shape
    sub = lax.broadcasted_iota(I32, (SUBLANES, n), 0)
    tiles = [s[v * SUBLANES:(v + 1) * SUBLANES, :] for v in range(r // SUBLANES)]
    rows = [sub + v * SUBLANES for v in range(r // SUBLANES)]
    slot = lax.broadcasted_iota(I32, (k, n), 0)
    vals = jnp.zeros((k, n), F32)
    idxs = jnp.zeros((k, n), I32)
    for j in range(k):
        level = list(zip(tiles, rows))
        while len(level) > 1:
            merged = [(jnp.maximum(va, vb), jnp.where(vb > va, ib, ia))
                      for (va, ia), (vb, ib) in zip(level[0::2], level[1::2])]
            level = merged + level[len(level) & ~1:]
        v8, i8 = level[0]
        m = jnp.max(v8, axis=0, keepdims=True)
        i = jnp.min(jnp.where(v8 == m, i8, r), axis=0, keepdims=True)
        vals = jnp.where(slot == j, m, vals)
        idxs = jnp.where(slot == j, i, idxs)
        tiles = [jnp.where(ri == i, -jnp.inf, t) for t, ri in zip(tiles, rows)]
    return vals, idxs


def _pick_rows(table, sel):
    out = jnp.zeros(sel.shape, table.dtype)
    for r in range(table.shape[0]):
        out = jnp.where(sel == r, table[r:r + 1, :], out)
    return out


_HALF_K = PEER_TOPK // 2
assert PEER_TOPK == 16 and SUBLANES == _HALF_K
_CAND_MID = PEER_TOPK + (_HALF_K - 1) * _HALF_K


def _candidates(s1, s2):
    parts = [s1[0:1, :] + s2]
    parts += [s1[i:i + 1, :] + s2[0:_HALF_K, :] for i in range(1, _HALF_K)]
    parts.append(s1[_HALF_K:, :] + s2[0:1, :])
    return jnp.concatenate(parts, axis=0)


def _candidate_ij(pos):
    mid = pos - PEER_TOPK
    i = jnp.where(pos < PEER_TOPK, 0, jnp.where(pos < _CAND_MID, (mid >> 3) + 1, pos - (_CAND_MID - _HALF_K)))
    j = jnp.where(pos < PEER_TOPK, pos, jnp.where(pos < _CAND_MID, mid & (_HALF_K - 1), 0))
    return i, j


def _route_body(hf_ref, wq_ref, keys_ref, e_ref, g_ref):
    tq = hf_ref.shape[0]
    half = PEER_DQ // 2
    q = jnp.dot(hf_ref[...], wq_ref[...], preferred_element_type=F32).astype(BF16)
    for cb in range(tq // LANES):
        tok = slice(cb * LANES, (cb + 1) * LANES)
        e_heads, g_heads = [], []
        for hd in range(PEER_HEADS):
            sub = []
            for part in range(2):
                col = (hd * 2 + part) * half
                st = lax.dot_general(keys_ref[hd, part], q[tok, col:col + half], _NT,
                                     preferred_element_type=F32)
                sub.append(_topk_rows(st, PEER_TOPK))
            (s1, i1), (s2, i2) = sub
            best, pos = _topk_rows(_candidates(s1, s2), PEER_TOPK)
            ci, cj = _candidate_ij(pos)
            e1 = _pick_rows(i1, ci)
            e2 = _pick_rows(i2, cj)
            ex = jnp.exp(best - best[0:1, :])
            e_heads.append((e1 * PEER_NKEYS + e2) * PACKED_ROWS)
            g_heads.append(ex / jnp.sum(ex, axis=0, keepdims=True))
        e_ref[tok, :] = jnp.concatenate(e_heads, axis=0).T
        g_ref[tok, :] = jnp.concatenate(g_heads, axis=0).T


def route_stage(hf, wq_bf, keys_bf, tq=256):
    n, d = hf.shape
    rows = lambda w: pl.BlockSpec((tq, w), lambda i: (i, 0))
    return pl.pallas_call(
        _route_body,
        grid=(n // tq,),
        in_specs=[rows(d), pl.BlockSpec(wq_bf.shape, lambda i: (0, 0)),
                  pl.BlockSpec(keys_bf.shape, lambda i: (0, 0, 0, 0))],
        out_specs=[rows(PEER_SLOTS), rows(PEER_SLOTS)],
        out_shape=[jax.ShapeDtypeStruct((n, PEER_SLOTS), I32),
                   jax.ShapeDtypeStruct((n, PEER_SLOTS), F32)],
        compiler_params=_params(("parallel",), VMEM_LIMIT),
        name="route",
    )(hf, wq_bf, keys_bf)


def _pack_body(t_ref, o_ref, tile_scr):
    for r in range(t_ref.shape[0] // SUBLANES):
        for c in range(ROW_TILES):
            dst = pl.ds(r * SUBLANES * ROW_TILES + c, SUBLANES, stride=ROW_TILES)
            tile_scr[dst, :] = t_ref[r * SUBLANES:(r + 1) * SUBLANES, c * LANES:(c + 1) * LANES]
    o_ref[...] = pltpu.bitcast(tile_scr[...].astype(BF16), jnp.uint32)


def pack_stage(t, te=128):
    return pl.pallas_call(
        _pack_body,
        grid=(PEER_EXPERTS // te,),
        in_specs=[pl.BlockSpec((te, D_MODEL), lambda i: (i, 0))],
        out_specs=pl.BlockSpec((te * PACKED_ROWS, LANES), lambda i: (i, 0)),
        out_shape=jax.ShapeDtypeStruct((PEER_EXPERTS * PACKED_ROWS, LANES), jnp.uint32),
        scratch_shapes=[pltpu.VMEM((te * ROW_TILES, LANES), F32)],
        compiler_params=_params(("parallel",)),
        name="pack",
    )(t)


def _table_tile(words):
    return pltpu.bitcast(words, BF16).astype(F32)


OFF_BITS = 16
assert PEER_EXPERTS * PACKED_ROWS <= 1 << OFF_BITS
PEER_TB = 128
_HALF_TB = PEER_TB // 2
_HALF_WORDS = _HALF_TB * PEER_SLOTS


def _offset_copy(off_hbm, bufs, sems, step, h):
    start = (2 * step + h) * _HALF_WORDS
    return pltpu.make_async_copy(off_hbm.at[pl.ds(start, _HALF_WORDS)], bufs[h], sems.at[h])


def _for_each_half(off_hbm, bufs, sems, half_fn):
    step = pl.program_id(0)

    @pl.when(step == 0)
    def _():
        _offset_copy(off_hbm, bufs, sems, step, 0).start()

    _offset_copy(off_hbm, bufs, sems, step, 1).start()
    _offset_copy(off_hbm, bufs, sems, step, 0).wait()
    half_fn(bufs[0], 0)

    @pl.when(step + 1 < pl.num_programs(0))
    def _():
        _offset_copy(off_hbm, bufs, sems, step + 1, 0).start()

    _offset_copy(off_hbm, bufs, sems, step, 1).wait()
    half_fn(bufs[1], _HALF_TB)


def _gather_row(tbl_ref, off):
    return tbl_ref[pl.ds(pl.multiple_of(off, PACKED_ROWS), PACKED_ROWS), :]


_OFFSET_SCRATCH = [pltpu.SMEM((_HALF_WORDS,), I32), pltpu.SMEM((_HALF_WORDS,), I32),
                   pltpu.SemaphoreType.DMA((2,))]


def _peer_u_body(off_hbm, h_ref, g_ref, tbl_ref, o_ref, idx_a, idx_b, sems, s_scr):
    eye = (lax.broadcasted_iota(I32, (PEER_SLOTS, LANES), 0)
           == lax.broadcasted_iota(I32, (PEER_SLOTS, LANES), 1))
    octet = SUBLANES * SUBLANES
    sel = (lax.broadcasted_iota(I32, (SUBLANES, octet), 1) // SUBLANES
           == lax.broadcasted_iota(I32, (SUBLANES, octet), 0)).astype(BF16)

    def half(idx_ref, first):
        for t in range(_HALF_TB):
            tok = first + t
            hv = h_ref[pl.ds(tok * ROW_TILES, ROW_TILES), :]
            for j in range(PEER_SLOTS // SUBLANES):
                prods = [_table_tile(_gather_row(tbl_ref, idx_ref[t * PEER_SLOTS + k])) * hv
                         for k in range(j * SUBLANES, (j + 1) * SUBLANES)]
                stack = jnp.concatenate(prods, axis=0)
                acc = jnp.dot(sel, stack.astype(BF16), preferred_element_type=F32)
                s_scr[pl.ds(tok * PEER_SLOTS + j * SUBLANES, SUBLANES), :] = acc
        for r0 in range(first, first + _HALF_TB, SUBLANES):
            acts = []
            for i in range(SUBLANES):
                tot = jnp.sum(s_scr[pl.ds((r0 + i) * PEER_SLOTS, PEER_SLOTS), :], axis=1, keepdims=True)
                acts.append(jnp.sum(jnp.where(eye, tot, 0.0), axis=0, keepdims=True))
            act = jnp.concatenate(acts, axis=0)
            rows8 = pl.ds(r0, SUBLANES)
            o_ref[rows8, :] = g_ref[rows8, :] * (0.5 * act * (1.0 + lax.erf(act * (2.0 ** -0.5))))

    _for_each_half(off_hbm, (idx_a, idx_b), sems, half)


def _token_tiles():
    return pl.BlockSpec((PEER_TB * ROW_TILES, LANES), lambda i: (i, 0))


def peer_u_stage(eoff, hf_tiles, gates, tbl):
    n = eoff.shape[0]
    rows = lambda: pl.BlockSpec((PEER_TB, PEER_SLOTS), lambda i: (i, 0))
    return pl.pallas_call(
        _peer_u_body,
        grid=(n // PEER_TB,),
        in_specs=[pl.BlockSpec(memory_space=pl.ANY), _token_tiles(), rows(), _resident(tbl.shape)],
        out_specs=rows(),
        out_shape=jax.ShapeDtypeStruct((n, PEER_SLOTS), F32),
        scratch_shapes=_OFFSET_SCRATCH + [pltpu.VMEM((PEER_TB * PEER_SLOTS, LANES), F32)],
        compiler_params=_params(("arbitrary",), VMEM_LIMIT),
        name="peer_u",
    )(eoff.reshape(-1), hf_tiles, gates, tbl)


_V_ACCS = 4


def _peer_v_body(off_hbm, w_ref, x_ref, g2_ref, fg_ref, tbl_ref, o_ref, idx_a, idx_b, sems, acc_scr):
    def half(idx_ref, first):
        for t in range(_HALF_TB):
            tok = first + t
            wrep = jnp.broadcast_to(w_ref[tok], (LANES, PEER_SLOTS)).T
            accs = [None] * _V_ACCS
            for k in range(PEER_SLOTS):
                row = _table_tile(_gather_row(tbl_ref, idx_ref[t * PEER_SLOTS + k]))
                term = jnp.broadcast_to(wrep[k:k + 1, :], (ROW_TILES, LANES)) * row
                accs[k % _V_ACCS] = term if accs[k % _V_ACCS] is None else accs[k % _V_ACCS] + term
            acc_scr[pl.ds(tok * ROW_TILES, ROW_TILES), :] = (accs[0] + accs[1]) + (accs[2] + accs[3])
        rows = pl.ds(first, _HALF_TB)
        x2 = []
        for c in range(ROW_TILES):
            block = pl.ds(first * ROW_TILES + c, _HALF_TB, stride=ROW_TILES)
            x2.append(x_ref[block, :] + g2_ref[c:c + 1, :] * acc_scr[block, :])
        ms = sum(jnp.sum(v * v, axis=1, keepdims=True) for v in x2) / D_MODEL
        inv = lax.rsqrt(ms + EPS)
        for c in range(ROW_TILES):
            o_ref[rows, c * LANES:(c + 1) * LANES] = (x2[c] * inv) * fg_ref[c:c + 1, :]

    _for_each_half(off_hbm, (idx_a, idx_b), sems, half)


def peer_v_stage(eoff, w, x_tiles, g2, final_g, tbl, tokens_per_batch):
    n = eoff.shape[0]
    per_b = tokens_per_batch // PEER_TB
    return pl.pallas_call(
        _peer_v_body,
        grid=(n // PEER_TB,),
        in_specs=[pl.BlockSpec(memory_space=pl.ANY),
                  pl.BlockSpec((PEER_TB, 1, PEER_SLOTS), lambda i: (i, 0, 0)), _token_tiles(),
                  pl.BlockSpec((None, ROW_TILES, LANES), lambda i: (i // per_b, 0, 0)),
                  pl.BlockSpec((ROW_TILES, LANES), lambda i: (0, 0)),
                  _resident(tbl.shape)],
        out_specs=pl.BlockSpec((PEER_TB, D_MODEL), lambda i: (i, 0)),
        out_shape=jax.ShapeDtypeStruct((n, D_MODEL), F32),
        scratch_shapes=_OFFSET_SCRATCH + [pltpu.VMEM((PEER_TB * ROW_TILES, LANES), F32)],
        compiler_params=_params(("arbitrary",), VMEM_LIMIT),
        name="peer_v",
    )(eoff.reshape(-1), w.reshape(n, 1, PEER_SLOTS), x_tiles, g2, final_g.reshape(ROW_TILES, LANES), tbl)


def _rope_tables(l):
    quarter = RET_DK // 4
    rows = l // GRID_W
    row = jnp.repeat(jnp.arange(rows, dtype=F32), GRID_W)
    col = jnp.tile(jnp.arange(GRID_W, dtype=F32), rows)
    inv = ROPE_BASE ** (-jnp.arange(quarter, dtype=F32) / quarter)
    ang = jnp.concatenate([row[:, None] * inv, col[:, None] * inv], axis=-1)
    cos, sin = jnp.cos(ang), jnp.sin(ang)
    return jnp.concatenate([cos, cos], axis=-1), jnp.concatenate([-sin, sin], axis=-1)


def _layer(x, ctx, c, c_ctx, ada_w, ada_b, norm_mix_g, norm_ffn_g, w_in, pool_w, pool_scale,
           pool_out, ret_decay, ret_norm_g, ret_out, w_out, peer_wq, peer_keys, peer_u, peer_v,
           final_g):
    b, l, d = x.shape
    n = b * l

    rows = -(-(b + 1) // SUBLANES) * SUBLANES
    cc = jnp.zeros((rows, d), F32).at[:b].set(c).at[b].set(c_ctx)
    mod = ada_stage(cc, ada_w, ada_b)
    sh1, sc1, g1, sh2, sc2, g2 = [m.reshape(b, 1, d) for m in jnp.split(mod[:b], 6, axis=-1)]
    csh1, csc1 = mod[b, :d], mod[b, d:2 * d]
    lg = jax.nn.log_sigmoid(ret_decay.astype(F32))

    w_in_bf = w_in.astype(BF16)
    s_f, s_b = ctx_stage(lg, ctx, norm_mix_g, csh1, csc1, w_in_bf[:, OFF_K:OFF_G])

    cos, sin = _rope_tables(l)
    p, q, k, v, gz, mg = proj_stage(x, norm_mix_g, sh1, sc1, cos, sin, w_in_bf)
    y = ret_stage(lg, q, k, v, s_f, s_b)
    hf_bf, x1_tiles, hf_tiles = mix_stage(p, y, gz, mg, x, g1, sh2, sc2, pool_w.astype(BF16), pool_scale,
                                          pool_out.astype(BF16), ret_norm_g, ret_out.astype(BF16),
                                          w_out.astype(BF16), norm_ffn_g)

    eoff, gates = route_stage(hf_bf.reshape(n, d), peer_wq.astype(BF16), peer_keys.astype(BF16))
    w = peer_u_stage(eoff, hf_tiles, gates, pack_stage(peer_u))
    out = peer_v_stage(eoff, w, x1_tiles, g2.reshape(b, ROW_TILES, LANES), final_g, pack_stage(peer_v), l)
    return out.reshape(b, l, d)


def kernel(x, c, ctx, c_ctx, ada_w, ada_b, norm_mix_g, norm_ffn_g, w_in, pool_w, pool_scale, pool_out, ret_decay, ret_norm_g, ret_out, w_out, peer_wq, peer_keys, peer_u, peer_v, final_g):
    assert ada_w.shape[0] == 1, "single-layer block"
    return _layer(x, ctx, c, c_ctx, ada_w[0], ada_b[0], norm_mix_g[0], norm_ffn_g[0], w_in[0],
                  pool_w[0], pool_scale[0], pool_out[0], ret_decay[0], ret_norm_g[0], ret_out[0],
                  w_out[0], peer_wq[0], peer_keys[0], peer_u[0], peer_v[0], final_g)
```

```python
import functools

import jax
import jax.numpy as jnp
import numpy as np
from jax import lax
from jax.experimental import pallas as pl
from jax.experimental.pallas import tpu as pltpu

F32 = jnp.float32
BF16 = jnp.bfloat16
I32 = jnp.int32

D_MODEL = 1024
GRID_W = 64
EPS = 1e-6

POOL_WINDOWS = (2, 4, 8, 16)
POOL_WIDTH = D_MODEL // 2
POOL_GROUP = POOL_WIDTH // len(POOL_WINDOWS)

RET_HEADS = 4
RET_DK = 128
RET_DV = 256
RET_CHUNK = 128
ROPE_BASE = 10000.0
QK_WIDTH = RET_HEADS * RET_DK
V_WIDTH = RET_HEADS * RET_DV
K_SCALE = RET_DK ** -0.5

OFF_POOL = 0
OFF_Q = OFF_POOL + POOL_WIDTH
OFF_K = OFF_Q + QK_WIDTH
OFF_V = OFF_K + QK_WIDTH
OFF_G = OFF_V + V_WIDTH
OFF_MERGE = OFF_G + V_WIDTH
IN_WIDTH = OFF_MERGE + 2 * D_MODEL

PEER_HEADS = 8
PEER_NKEYS = 128
PEER_EXPERTS = PEER_NKEYS * PEER_NKEYS
PEER_TOPK = 16
PEER_DQ = 256
PEER_SLOTS = PEER_HEADS * PEER_TOPK

LANES = 128
SUBLANES = 8
ROW_TILES = D_MODEL // LANES
PACKED_ROWS = ROW_TILES // 2
VMEM_LIMIT = 56 * 1024 * 1024

_NT = (((1,), (1,)), ((), ()))
_TN = (((0,), (0,)), ((), ()))


def _params(sem, vmem=None):
    return pltpu.CompilerParams(dimension_semantics=sem, vmem_limit_bytes=vmem)


def _resident(shape):
    nd = len(shape)
    return pl.BlockSpec(shape, lambda *_: (0,) * nd, pipeline_mode=pl.Buffered(1))


def _rms_mod(xf, g, shift, scale):
    y = xf * lax.rsqrt(jnp.mean(xf * xf, axis=-1, keepdims=True) + EPS)
    return (y * g) * (1.0 + scale) + shift


def _ada_body(c_ref, w_ref, b_ref, o_ref):
    c = c_ref[...]
    s = c * jax.nn.sigmoid(c)
    o_ref[...] = jnp.dot(s, w_ref[...], preferred_element_type=F32,
                         precision=lax.Precision.HIGHEST) + b_ref[...]


def ada_stage(cc, ada_w, ada_b, tn=512):
    r, d = cc.shape
    n = ada_w.shape[1]
    return pl.pallas_call(
        _ada_body,
        grid=(n // tn,),
        in_specs=[pl.BlockSpec((r, d), lambda j: (0, 0)),
                  pl.BlockSpec((d, tn), lambda j: (0, j)),
                  pl.BlockSpec((1, tn), lambda j: (0, j))],
        out_specs=pl.BlockSpec((r, tn), lambda j: (0, j)),
        out_shape=jax.ShapeDtypeStruct((r, n), F32),
        compiler_params=_params(("parallel",)),
        name="ada",
    )(cc, ada_w, ada_b.reshape(1, n))


def _ctx_body(lg_ref, ctx_ref, g_ref, sh_ref, sc_ref, w_ref, sf_ref, sb_ref):
    lc = ctx_ref.shape[0]
    hc = _rms_mod(ctx_ref[...], g_ref[...], sh_ref[...], sc_ref[...])
    kv = jnp.dot(hc.astype(BF16), w_ref[...], preferred_element_type=F32)
    m = lax.broadcasted_iota(I32, (lc, RET_DK), 0).astype(F32)
    for h in range(RET_HEADS):
        kf = kv[:, h * RET_DK:(h + 1) * RET_DK] * K_SCALE
        vb = kv[:, QK_WIDTH + h * RET_DV:QK_WIDTH + (h + 1) * RET_DV].astype(BF16)
        wf = jnp.exp(lg_ref[0, h] * (lc - 1.0 - m))
        wb = jnp.exp(lg_ref[1, h] * m)
        sf_ref[h] = lax.dot_general((kf * wf).astype(BF16), vb, _TN, preferred_element_type=F32)
        sb_ref[h] = lax.dot_general((kf * wb).astype(BF16), vb, _TN, preferred_element_type=F32)


def ctx_stage(lg, ctx, norm_g, csh, csc, w_kv):
    b, lc, d = ctx.shape
    vec = pl.BlockSpec((1, d), lambda i: (0, 0))
    st = jax.ShapeDtypeStruct((b, RET_HEADS, RET_DK, RET_DV), F32)
    st_spec = pl.BlockSpec((None, RET_HEADS, RET_DK, RET_DV), lambda i: (i, 0, 0, 0))
    return pl.pallas_call(
        _ctx_body,
        grid=(b,),
        in_specs=[pl.BlockSpec(memory_space=pltpu.SMEM),
                  pl.BlockSpec((None, lc, d), lambda i: (i, 0, 0)),
                  vec, vec, vec,
                  pl.BlockSpec(w_kv.shape, lambda i: (0, 0))],
        out_specs=[st_spec, st_spec],
        out_shape=[st, st],
        compiler_params=_params(("parallel",)),
        name="ctx",
    )(lg, ctx, norm_g.reshape(1, d), csh.reshape(1, d), csc.reshape(1, d), w_kv)


def _rope(a, cos, sin_signed):
    return a * cos + pltpu.roll(a, RET_DK // 2, 1) * sin_signed


def _proj_body(x_ref, g_ref, sh_ref, sc_ref, cos_ref, sin_ref, w_ref,
               p_ref, q_ref, k_ref, v_ref, gz_ref, mg_ref):
    h = _rms_mod(x_ref[...], g_ref[...], sh_ref[...], sc_ref[...]).astype(BF16)

    def mm(lo, hi):
        return jnp.dot(h, w_ref[:, lo:hi], preferred_element_type=F32)

    p_ref[...] = mm(OFF_POOL, OFF_Q)
    cos = cos_ref[...]
    sin = sin_ref[...]
    qf = mm(OFF_Q, OFF_K)
    kf = mm(OFF_K, OFF_V)
    for hd in range(RET_HEADS):
        sl = slice(hd * RET_DK, (hd + 1) * RET_DK)
        q_ref[:, sl] = _rope(qf[:, sl], cos, sin).astype(BF16)
        k_ref[:, sl] = (_rope(kf[:, sl], cos, sin) * K_SCALE).astype(BF16)
    v_ref[...] = mm(OFF_V, OFF_G).astype(BF16)
    gz_ref[...] = mm(OFF_G, OFF_MERGE).astype(BF16)
    mg_ref[...] = mm(OFF_MERGE, IN_WIDTH).astype(BF16)


def proj_stage(x, norm_g, sh1, sc1, cos, sin, w_in_bf, tm=512):
    b, l, d = x.shape
    vec_b = pl.BlockSpec((None, 1, d), lambda i, j: (i, 0, 0))
    rows = lambda w: pl.BlockSpec((None, tm, w), lambda i, j: (i, j, 0))
    tab = pl.BlockSpec((tm, RET_DK), lambda i, j: (j, 0))
    outs = [(POOL_WIDTH, F32), (QK_WIDTH, BF16), (QK_WIDTH, BF16), (V_WIDTH, BF16),
            (V_WIDTH, BF16), (2 * D_MODEL, BF16)]
    return pl.pallas_call(
        _proj_body,
        grid=(b, l // tm),
        in_specs=[rows(d), pl.BlockSpec((1, d), lambda i, j: (0, 0)), vec_b, vec_b, tab, tab,
                  _resident(w_in_bf.shape)],
        out_specs=[rows(w) for w, _ in outs],
        out_shape=[jax.ShapeDtypeStruct((b, l, w), dt) for w, dt in outs],
        compiler_params=_params(("parallel", "parallel"), VMEM_LIMIT),
        name="proj",
    )(x, norm_g.reshape(1, d), sh1, sc1, cos, sin, w_in_bf)


_RET_GROUP = 2


def _ret_body(lg_ref, q_ref, k_ref, v_ref, sf_ref, sb_ref, y_ref, sfw_scr, sbw_scr, yb_scr):
    c = RET_CHUNK
    n_chunks = q_ref.shape[0] // c
    n_i = lax.broadcasted_iota(I32, (c, c), 0)
    m_i = lax.broadcasted_iota(I32, (c, c), 1)
    rel = (n_i - m_i).astype(F32)
    pos = lax.broadcasted_iota(I32, (c, RET_DK), 0).astype(F32)

    def decays(hh):
        hd = pl.program_id(1) * _RET_GROUP + hh
        lgf = lg_ref[0, hd]
        lgb = lg_ref[1, hd]
        fwd = (jnp.where(rel >= 0, jnp.exp(lgf * jnp.where(rel >= 0, rel, 0.0)), 0.0),
               jnp.exp(lgf * (pos + 1.0)), jnp.exp(lgf * (c - 1.0 - pos)),
               jnp.exp(jnp.full((1, RET_DV), lgf * c, F32)))
        bwd = (jnp.where(rel < 0, jnp.exp(lgb * jnp.where(rel < 0, -rel, 0.0)), 0.0),
               jnp.exp(lgb * (c - pos)), jnp.exp(lgb * pos),
               jnp.exp(jnp.full((1, RET_DV), lgb * c, F32)))
        return fwd, bwd

    tables = [decays(hh) for hh in range(_RET_GROUP)]

    def chunk(i, hh, st_ref, intra, qdec, kdec, blk):
        rows = pl.ds(pl.multiple_of(i * c, c), c)
        qk_cols = slice(hh * RET_DK, (hh + 1) * RET_DK)
        v_cols = slice(hh * RET_DV, (hh + 1) * RET_DV)
        qi = q_ref[rows, qk_cols]
        ki = k_ref[rows, qk_cols]
        vi = v_ref[rows, v_cols]
        sc = lax.dot_general(qi, ki, _NT, preferred_element_type=F32) * intra
        s = st_ref[hh]
        y = (jnp.dot(sc.astype(BF16), vi, preferred_element_type=F32)
             + jnp.dot((qi.astype(F32) * qdec).astype(BF16), s.astype(BF16),
                       preferred_element_type=F32))
        st_ref[hh] = s * blk + lax.dot_general((ki.astype(F32) * kdec).astype(BF16), vi, _TN,
                                               preferred_element_type=F32)
        return rows, v_cols, y

    sfw_scr[...] = sf_ref[...]
    sbw_scr[...] = sb_ref[...]

    def step(i, carry):
        for hh, (fwd, bwd) in enumerate(tables):
            rows, cols, y = chunk(i, hh, sfw_scr, *fwd)
            y_ref[rows, cols] = y
            rows, cols, y = chunk(n_chunks - 1 - i, hh, sbw_scr, *bwd)
            yb_scr[rows, cols] = y
        return carry

    lax.fori_loop(0, n_chunks, step, 0)
    y_ref[...] += yb_scr[...]


def ret_stage(lg, q, k, v, s_f, s_b):
    b, l, _ = q.shape
    g = _RET_GROUP
    qk_spec = pl.BlockSpec((None, l, g * RET_DK), lambda i, j: (i, 0, j))
    v_spec = pl.BlockSpec((None, l, g * RET_DV), lambda i, j: (i, 0, j))
    st_spec = pl.BlockSpec((None, g, RET_DK, RET_DV), lambda i, j: (i, j, 0, 0))
    return pl.pallas_call(
        _ret_body,
        grid=(b, RET_HEADS // g),
        in_specs=[pl.BlockSpec(memory_space=pltpu.SMEM), qk_spec, qk_spec, v_spec, st_spec, st_spec],
        out_specs=v_spec,
        out_shape=jax.ShapeDtypeStruct((b, l, V_WIDTH), F32),
        scratch_shapes=[pltpu.VMEM((g, RET_DK, RET_DV), F32), pltpu.VMEM((g, RET_DK, RET_DV), F32),
                        pltpu.VMEM((l, g * RET_DV), F32)],
        compiler_params=_params(("parallel", "parallel")),
        name="ret",
    )(lg, q, k, v, s_f, s_b)


def _pool_bands(tm):
    r = np.arange(tm)[:, None]
    c = np.arange(tm)[None, :]
    bands = np.zeros((len(POOL_WINDOWS), 3, tm, tm), np.float32)
    for gi, w in enumerate(POOL_WINDOWS):
        lo, hi = r - w // 2, r + w - w // 2
        for j, off in enumerate((-tm, 0, tm)):
            bands[gi, j] = ((c + off >= lo) & (c + off < hi)).astype(np.float32)
    return jnp.asarray(bands, BF16)


def _mix_body(pp_ref, pm_ref, pn_ref, y_ref, gz_ref, mg_ref, x_ref, g1_ref, sh2_ref, sc2_ref,
              band_ref, pw_ref, ps_ref, po_ref, rg_ref, ro_ref, wo_ref, ng_ref,
              hf_ref, x1t_ref, hft_ref, *, seq_len):
    tm = pm_ref.shape[0]
    li = pl.program_id(1)
    has_prev = (li > 0).astype(F32)
    has_next = (li < pl.num_programs(1) - 1).astype(F32)
    t = li * tm + lax.broadcasted_iota(I32, (tm, POOL_GROUP), 0)

    def window_sum(ref, cols, gi, j):
        pf = ref[:, cols]
        hi = pf.astype(BF16)
        lo = (pf - hi.astype(F32)).astype(BF16)
        band = band_ref[gi, j]
        return (jnp.dot(band, hi, preferred_element_type=F32)
                + jnp.dot(band, lo, preferred_element_type=F32))

    mixed = []
    for gi, w in enumerate(POOL_WINDOWS):
        cols = slice(gi * POOL_GROUP, (gi + 1) * POOL_GROUP)
        ws = (window_sum(pm_ref, cols, gi, 1) + has_prev * window_sum(pp_ref, cols, gi, 0)
              + has_next * window_sum(pn_ref, cols, gi, 2))
        cnt = (jnp.clip(t + (w - w // 2), 0, seq_len) - jnp.clip(t - w // 2, 0, seq_len)).astype(F32)
        dgi = ws / cnt - pm_ref[:, cols]
        mixed.append(jnp.dot(dgi.astype(BF16), pw_ref[gi], preferred_element_type=F32))
    mixed = jnp.concatenate(mixed, axis=1) * ps_ref[...]
    pool = jnp.dot(mixed.astype(BF16), po_ref[...], preferred_element_type=F32)

    yn = []
    for hd in range(RET_HEADS):
        yh = y_ref[:, hd * RET_DV:(hd + 1) * RET_DV]
        mu = jnp.mean(yh, axis=-1, keepdims=True)
        yc = yh - mu
        var = jnp.mean(yc * yc, axis=-1, keepdims=True)
        yn.append(yc * lax.rsqrt(var + EPS))
    yn = jnp.concatenate(yn, axis=1) * rg_ref[...]
    gate = gz_ref[...].astype(F32)
    ret = jnp.dot((yn * (gate * jax.nn.sigmoid(gate))).astype(BF16), ro_ref[...],
                  preferred_element_type=F32)

    g_pool = mg_ref[:, :D_MODEL].astype(F32)
    g_ret = mg_ref[:, D_MODEL:].astype(F32)
    merged = jax.nn.sigmoid(g_pool) * pool + jax.nn.sigmoid(g_ret) * ret
    out = jnp.dot(merged.astype(BF16), wo_ref[...], preferred_element_type=F32)
    x1 = x_ref[...] + g1_ref[...] * out
    hf = _rms_mod(x1, ng_ref[...], sh2_ref[...], sc2_ref[...])
    hf_ref[...] = hf.astype(BF16)
    for r in range(tm // SUBLANES):
        for c in range(ROW_TILES):
            dst = pl.ds(r * SUBLANES * ROW_TILES + c, SUBLANES, stride=ROW_TILES)
            src = (slice(r * SUBLANES, (r + 1) * SUBLANES), slice(c * LANES, (c + 1) * LANES))
            x1t_ref[dst, :] = x1[src]
            hft_ref[dst, :] = hf[src]


def mix_stage(p, y, gz, mg, x, g1, sh2, sc2, pool_w_bf, pool_scale, pool_out_bf, ret_norm_g,
              ret_out_bf, w_out_bf, norm_ffn_g, tm=256):
    b, l, d = x.shape
    nl = l // tm
    bands = _pool_bands(tm)
    rows = lambda w: pl.BlockSpec((None, tm, w), lambda i, j: (i, j, 0))
    vec_b = pl.BlockSpec((None, 1, d), lambda i, j: (i, 0, 0))
    const = lambda a: pl.BlockSpec(a.shape, lambda i, j: (0,) * a.ndim)
    tiles = pl.BlockSpec((tm * ROW_TILES, LANES), lambda i, j: (i * nl + j, 0))
    ps = pool_scale.reshape(1, POOL_WIDTH)
    rg = ret_norm_g.reshape(1, V_WIDTH)
    ng = norm_ffn_g.reshape(1, d)
    return pl.pallas_call(
        functools.partial(_mix_body, seq_len=l),
        grid=(b, nl),
        in_specs=[pl.BlockSpec((None, tm, POOL_WIDTH), lambda i, j: (i, jnp.maximum(j - 1, 0), 0)),
                  rows(POOL_WIDTH),
                  pl.BlockSpec((None, tm, POOL_WIDTH), lambda i, j: (i, jnp.minimum(j + 1, nl - 1), 0)),
                  rows(V_WIDTH), rows(V_WIDTH), rows(2 * D_MODEL), rows(d), vec_b, vec_b, vec_b,
                  const(bands), const(pool_w_bf), const(ps), const(pool_out_bf), const(rg),
                  const(ret_out_bf), const(w_out_bf), const(ng)],
        out_specs=[rows(d), tiles, tiles],
        out_shape=[jax.ShapeDtypeStruct((b, l, d), BF16),
                   jax.ShapeDtypeStruct((b * l * ROW_TILES, LANES), F32),
                   jax.ShapeDtypeStruct((b * l * ROW_TILES, LANES), F32)],
        compiler_params=_params(("parallel", "parallel"), VMEM_LIMIT),
        name="mix",
    )(p, p, p, y, gz, mg, x, g1, sh2, sc2, bands, pool_w_bf, ps, pool_out_bf, rg, ret_out_bf,
      w_out_bf, ng)


def _topk_rows(s, k):
    r, n = s.shape
    sub = lax.broadcasted_iota(I32, (SUBLANES, n), 0)
    tiles = [s[v * SUBLANES:(v + 1) * SUBLANES, :] for v in range(r // SUBLANES)]
    rows = [sub + v * SUBLANES for v in range(r // SUBLANES)]
    slot = lax.broadcasted_iota(I32, (k, n), 0)
    vals = jnp.zeros((k, n), F32)
    idxs = jnp.zeros((k, n), I32)
    for j in range(k):
        level = list(zip(tiles, rows))
        while len(level) > 1:
            merged = [(jnp.maximum(va, vb), jnp.where(vb > va, ib, ia))
                      for (va, ia), (vb, ib) in zip(level[0::2], level[1::2])]
            level = merged + level[len(level) & ~1:]
        v8, i8 = level[0]
        m = jnp.max(v8, axis=0, keepdims=True)
        i = jnp.min(jnp.where(v8 == m, i8, r), axis=0, keepdims=True)
        vals = jnp.where(slot == j, m, vals)
        idxs = jnp.where(slot == j, i, idxs)
        tiles = [jnp.where(ri == i, -jnp.inf, t) for t, ri in zip(tiles, rows)]
    return vals, idxs


def _pick_rows(table, sel):
    out = jnp.zeros(sel.shape, table.dtype)
    for r in range(table.shape[0]):
        out = jnp.where(sel == r, table[r:r + 1, :], out)
    return out


_HALF_K = PEER_TOPK // 2
assert PEER_TOPK == 16 and SUBLANES == _HALF_K
_CAND_MID = PEER_TOPK + (_HALF_K - 1) * _HALF_K


def _candidates(s1, s2):
    parts = [s1[0:1, :] + s2]
    parts += [s1[i:i + 1, :] + s2[0:_HALF_K, :] for i in range(1, _HALF_K)]
    parts.append(s1[_HALF_K:, :] + s2[0:1, :])
    return jnp.concatenate(parts, axis=0)


def _candidate_ij(pos):
    mid = pos - PEER_TOPK
    i = jnp.where(pos < PEER_TOPK, 0, jnp.where(pos < _CAND_MID, (mid >> 3) + 1, pos - (_CAND_MID - _HALF_K)))
    j = jnp.where(pos < PEER_TOPK, pos, jnp.where(pos < _CAND_MID, mid & (_HALF_K - 1), 0))
    return i, j


def _route_body(hf_ref, wq_ref, keys_ref, e_ref, g_ref):
    tq = hf_ref.shape[0]
    half = PEER_DQ // 2
    q = jnp.dot(hf_ref[...], wq_ref[...], preferred_element_type=F32).astype(BF16)
    for cb in range(tq // LANES):
        tok = slice(cb * LANES, (cb + 1) * LANES)
        e_heads, g_heads = [], []
        for hd in range(PEER_HEADS):
            sub = []
            for part in range(2):
                col = (hd * 2 + part) * half
                st = lax.dot_general(keys_ref[hd, part], q[tok, col:col + half], _NT,
                                     preferred_element_type=F32)
                sub.append(_topk_rows(st, PEER_TOPK))
            (s1, i1), (s2, i2) = sub
            best, pos = _topk_rows(_candidates(s1, s2), PEER_TOPK)
            ci, cj = _candidate_ij(pos)
            e1 = _pick_rows(i1, ci)
            e2 = _pick_rows(i2, cj)
            ex = jnp.exp(best - best[0:1, :])
            e_heads.append((e1 * PEER_NKEYS + e2) * PACKED_ROWS)
            g_heads.append(ex / jnp.sum(ex, axis=0, keepdims=True))
        e_ref[tok, :] = jnp.concatenate(e_heads, axis=0).T
        g_ref[tok, :] = jnp.concatenate(g_heads, axis=0).T


def route_stage(hf, wq_bf, keys_bf, tq=256):
    n, d = hf.shape
    rows = lambda w: pl.BlockSpec((tq, w), lambda i: (i, 0))
    return pl.pallas_call(
        _route_body,
        grid=(n // tq,),
        in_specs=[rows(d), pl.BlockSpec(wq_bf.shape, lambda i: (0, 0)),
                  pl.BlockSpec(keys_bf.shape, lambda i: (0, 0, 0, 0))],
        out_specs=[rows(PEER_SLOTS), rows(PEER_SLOTS)],
        out_shape=[jax.ShapeDtypeStruct((n, PEER_SLOTS), I32),
                   jax.ShapeDtypeStruct((n, PEER_SLOTS), F32)],
        compiler_params=_params(("parallel",), VMEM_LIMIT),
        name="route",
    )(hf, wq_bf, keys_bf)


def _pack_body(t_ref, o_ref, tile_scr):
    for r in range(t_ref.shape[0] // SUBLANES):
        for c in range(ROW_TILES):
            dst = pl.ds(r * SUBLANES * ROW_TILES + c, SUBLANES, stride=ROW_TILES)
            tile_scr[dst, :] = t_ref[r * SUBLANES:(r + 1) * SUBLANES, c * LANES:(c + 1) * LANES]
    o_ref[...] = pltpu.bitcast(tile_scr[...].astype(BF16), jnp.uint32)


def pack_stage(t, te=512):
    return pl.pallas_call(
        _pack_body,
        grid=(PEER_EXPERTS // te,),
        in_specs=[pl.BlockSpec((te, D_MODEL), lambda i: (i, 0))],
        out_specs=pl.BlockSpec((te * PACKED_ROWS, LANES), lambda i: (i, 0)),
        out_shape=jax.ShapeDtypeStruct((PEER_EXPERTS * PACKED_ROWS, LANES), jnp.uint32),
        scratch_shapes=[pltpu.VMEM((te * ROW_TILES, LANES), F32)],
        compiler_params=_params(("parallel",)),
        name="pack",
    )(t)


def _table_tile(words):
    return pltpu.bitcast(words, BF16).astype(F32)


OFF_BITS = 16
assert PEER_EXPERTS * PACKED_ROWS <= 1 << OFF_BITS
PEER_TB = 128
_HALF_TB = PEER_TB // 2
_HALF_WORDS = _HALF_TB * PEER_SLOTS


def _offset_copy(off_hbm, bufs, sems, step, h):
    start = (2 * step + h) * _HALF_WORDS
    return pltpu.make_async_copy(off_hbm.at[pl.ds(start, _HALF_WORDS)], bufs[h], sems.at[h])


def _for_each_half(off_hbm, bufs, sems, half_fn):
    step = pl.program_id(0)

    @pl.when(step == 0)
    def _():
        _offset_copy(off_hbm, bufs, sems, step, 0).start()

    _offset_copy(off_hbm, bufs, sems, step, 1).start()
    _offset_copy(off_hbm, bufs, sems, step, 0).wait()
    half_fn(bufs[0], 0)

    @pl.when(step + 1 < pl.num_programs(0))
    def _():
        _offset_copy(off_hbm, bufs, sems, step + 1, 0).start()

    _offset_copy(off_hbm, bufs, sems, step, 1).wait()
    half_fn(bufs[1], _HALF_TB)


def _gather_row(tbl_ref, off):
    return tbl_ref[pl.ds(pl.multiple_of(off, PACKED_ROWS), PACKED_ROWS), :]


_OFFSET_SCRATCH = [pltpu.SMEM((_HALF_WORDS,), I32), pltpu.SMEM((_HALF_WORDS,), I32),
                   pltpu.SemaphoreType.DMA((2,))]


def _peer_u_body(off_hbm, h_ref, g_ref, tbl_ref, o_ref, idx_a, idx_b, sems, s_scr):
    eye = (lax.broadcasted_iota(I32, (PEER_SLOTS, LANES), 0)
           == lax.broadcasted_iota(I32, (PEER_SLOTS, LANES), 1))
    octet = SUBLANES * SUBLANES
    sel = (lax.broadcasted_iota(I32, (SUBLANES, octet), 1) // SUBLANES
           == lax.broadcasted_iota(I32, (SUBLANES, octet), 0)).astype(BF16)

    def half(idx_ref, first):
        for t in range(_HALF_TB):
            tok = first + t
            hv = h_ref[pl.ds(tok * ROW_TILES, ROW_TILES), :]
            for j in range(PEER_SLOTS // SUBLANES):
                prods = [_table_tile(_gather_row(tbl_ref, idx_ref[t * PEER_SLOTS + k])) * hv
                         for k in range(j * SUBLANES, (j + 1) * SUBLANES)]
                stack = jnp.concatenate(prods, axis=0)
                acc = jnp.dot(sel, stack.astype(BF16), preferred_element_type=F32)
                s_scr[pl.ds(tok * PEER_SLOTS + j * SUBLANES, SUBLANES), :] = acc
        for r0 in range(first, first + _HALF_TB, SUBLANES):
            acts = []
            for i in range(SUBLANES):
                tot = jnp.sum(s_scr[pl.ds((r0 + i) * PEER_SLOTS, PEER_SLOTS), :], axis=1, keepdims=True)
                acts.append(jnp.sum(jnp.where(eye, tot, 0.0), axis=0, keepdims=True))
            act = jnp.concatenate(acts, axis=0)
            rows8 = pl.ds(r0, SUBLANES)
            o_ref[rows8, :] = g_ref[rows8, :] * (0.5 * act * (1.0 + lax.erf(act * (2.0 ** -0.5))))

    _for_each_half(off_hbm, (idx_a, idx_b), sems, half)


def _token_tiles():
    return pl.BlockSpec((PEER_TB * ROW_TILES, LANES), lambda i: (i, 0))


def peer_u_stage(eoff, hf_tiles, gates, tbl):
    n = eoff.shape[0]
    rows = lambda: pl.BlockSpec((PEER_TB, PEER_SLOTS), lambda i: (i, 0))
    return pl.pallas_call(
        _peer_u_body,
        grid=(n // PEER_TB,),
        in_specs=[pl.BlockSpec(memory_space=pl.ANY), _token_tiles(), rows(), _resident(tbl.shape)],
        out_specs=rows(),
        out_shape=jax.ShapeDtypeStruct((n, PEER_SLOTS), F32),
        scratch_shapes=_OFFSET_SCRATCH + [pltpu.VMEM((PEER_TB * PEER_SLOTS, LANES), F32)],
        compiler_params=_params(("arbitrary",), VMEM_LIMIT),
        name="peer_u",
    )(eoff.reshape(-1), hf_tiles, gates, tbl)


_V_ACCS = 4


def _peer_v_body(off_hbm, w_ref, x_ref, g2_ref, fg_ref, tbl_ref, o_ref, idx_a, idx_b, sems, acc_scr):
    def half(idx_ref, first):
        for t in range(_HALF_TB):
            tok = first + t
            wrep = jnp.broadcast_to(w_ref[tok], (LANES, PEER_SLOTS)).T
            accs = [None] * _V_ACCS
            for k in range(PEER_SLOTS):
                row = _table_tile(_gather_row(tbl_ref, idx_ref[t * PEER_SLOTS + k]))
                term = jnp.broadcast_to(wrep[k:k + 1, :], (ROW_TILES, LANES)) * row
                accs[k % _V_ACCS] = term if accs[k % _V_ACCS] is None else accs[k % _V_ACCS] + term
            acc_scr[pl.ds(tok * ROW_TILES, ROW_TILES), :] = (accs[0] + accs[1]) + (accs[2] + accs[3])
        rows = pl.ds(first, _HALF_TB)
        x2 = []
        for c in range(ROW_TILES):
            block = pl.ds(first * ROW_TILES + c, _HALF_TB, stride=ROW_TILES)
            x2.append(x_ref[block, :] + g2_ref[c:c + 1, :] * acc_scr[block, :])
        ms = sum(jnp.sum(v * v, axis=1, keepdims=True) for v in x2) / D_MODEL
        inv = lax.rsqrt(ms + EPS)
        for c in range(ROW_TILES):
            o_ref[rows, c * LANES:(c + 1) * LANES] = (x2[c] * inv) * fg_ref[c:c + 1, :]

    _for_each_half(off_hbm, (idx_a, idx_b), sems, half)


def peer_v_stage(eoff, w, x_tiles, g2, final_g, tbl, tokens_per_batch):
    n = eoff.shape[0]
    per_b = tokens_per_batch // PEER_TB
    return pl.pallas_call(
        _peer_v_body,
        grid=(n // PEER_TB,),
        in_specs=[pl.BlockSpec(memory_space=pl.ANY),
                  pl.BlockSpec((PEER_TB, 1, PEER_SLOTS), lambda i: (i, 0, 0)), _token_tiles(),
                  pl.BlockSpec((None, ROW_TILES, LANES), lambda i: (i // per_b, 0, 0)),
                  pl.BlockSpec((ROW_TILES, LANES), lambda i: (0, 0)),
                  _resident(tbl.shape)],
        out_specs=pl.BlockSpec((PEER_TB, D_MODEL), lambda i: (i, 0)),
        out_shape=jax.ShapeDtypeStruct((n, D_MODEL), F32),
        scratch_shapes=_OFFSET_SCRATCH + [pltpu.VMEM((PEER_TB * ROW_TILES, LANES), F32)],
        compiler_params=_params(("arbitrary",), VMEM_LIMIT),
        name="peer_v",
    )(eoff.reshape(-1), w.reshape(n, 1, PEER_SLOTS), x_tiles, g2, final_g.reshape(ROW_TILES, LANES), tbl)


def _rope_tables(l):
    quarter = RET_DK // 4
    rows = l // GRID_W
    row = jnp.repeat(jnp.arange(rows, dtype=F32), GRID_W)
    col = jnp.tile(jnp.arange(GRID_W, dtype=F32), rows)
    inv = ROPE_BASE ** (-jnp.arange(quarter, dtype=F32) / quarter)
    ang = jnp.concatenate([row[:, None] * inv, col[:, None] * inv], axis=-1)
    cos, sin = jnp.cos(ang), jnp.sin(ang)
    return jnp.concatenate([cos, cos], axis=-1), jnp.concatenate([-sin, sin], axis=-1)


def _layer(x, ctx, c, c_ctx, ada_w, ada_b, norm_mix_g, norm_ffn_g, w_in, pool_w, pool_scale,
           pool_out, ret_decay, ret_norm_g, ret_out, w_out, peer_wq, peer_keys, peer_u, peer_v,
           final_g):
    b, l, d = x.shape
    n = b * l

    rows = -(-(b + 1) // SUBLANES) * SUBLANES
    cc = jnp.zeros((rows, d), F32).at[:b].set(c).at[b].set(c_ctx)
    mod = ada_stage(cc, ada_w, ada_b)
    sh1, sc1, g1, sh2, sc2, g2 = [m.reshape(b, 1, d) for m in jnp.split(mod[:b], 6, axis=-1)]
    csh1, csc1 = mod[b, :d], mod[b, d:2 * d]
    lg = jax.nn.log_sigmoid(ret_decay.astype(F32))

    w_in_bf = w_in.astype(BF16)
    s_f, s_b = ctx_stage(lg, ctx, norm_mix_g, csh1, csc1, w_in_bf[:, OFF_K:OFF_G])

    cos, sin = _rope_tables(l)
    p, q, k, v, gz, mg = proj_stage(x, norm_mix_g, sh1, sc1, cos, sin, w_in_bf)
    y = ret_stage(lg, q, k, v, s_f, s_b)
    hf_bf, x1_tiles, hf_tiles = mix_stage(p, y, gz, mg, x, g1, sh2, sc2, pool_w.astype(BF16), pool_scale,
                                          pool_out.astype(BF16), ret_norm_g, ret_out.astype(BF16),
                                          w_out.astype(BF16), norm_ffn_g)

    eoff, gates = route_stage(hf_bf.reshape(n, d), peer_wq.astype(BF16), peer_keys.astype(BF16))
    w = peer_u_stage(eoff, hf_tiles, gates, pack_stage(peer_u))
    out = peer_v_stage(eoff, w, x1_tiles, g2.reshape(b, ROW_TILES, LANES), final_g, pack_stage(peer_v), l)
    return out.reshape(b, l, d)


def kernel(x, c, ctx, c_ctx, ada_w, ada_b, norm_mix_g, norm_ffn_g, w_in, pool_w, pool_scale, pool_out, ret_decay, ret_norm_g, ret_out, w_out, peer_wq, peer_keys, peer_u, peer_v, final_g):
    assert ada_w.shape[0] == 1, "single-layer block"
    return _layer(x, ctx, c, c_ctx, ada_w[0], ada_b[0], norm_mix_g[0], norm_ffn_g[0], w_in[0],
                  pool_w[0], pool_scale[0], pool_out[0], ret_decay[0], ret_norm_g[0], ret_out[0],
                  w_out[0], peer_wq[0], peer_keys[0], peer_u[0], peer_v[0], final_g)
```
